```python
import jax
import jax.numpy as jnp
from jax import lax
import numpy as np

D_MODEL = 1024
BATCH = 32
SEQ = 256
DEPTH = 4
DEC_BATCH = 4
DEC_SEQ = 1024
PAST_LEN = 512

GRID_W = 64
N_MIXERS = 4
N_A = (DEPTH + 3) // N_MIXERS
N_B = (DEPTH + 2) // N_MIXERS
N_C = (DEPTH + 1) // N_MIXERS
N_D = DEPTH // N_MIXERS

HEADS_A = 16
KV_HEADS_A = 4
HEAD_DIM_A = D_MODEL // HEADS_A
ROPE_THETA = 10000.0
Q_BLOCK = 128
HEADS_B = 8
DK_B = D_MODEL // HEADS_B
DV_B = D_MODEL // HEADS_B
QKV_B = 2 * HEADS_B * DK_B + HEADS_B * DV_B
CONV_W = 3
GDN_CHUNK = 64
HEADS_C = 16
HEAD_DIM_C = D_MODEL // HEADS_C
WIN_R = 8
WIN_C = 16
HEADS_D = 8
DV_D = D_MODEL // HEADS_D
DK_D = DV_D // 2
MLSTM_CHUNK = 64
N_EXPERTS = 32
TOP_K = 4
D_FF = D_MODEL
SWIGLU_LIMIT = 7.0
SWIGLU_ALPHA = 1.702
EXPERT_BLOCK = 128

DEEPNORM_ALPHA = (2 * DEPTH) ** 0.25
DEEPNORM_BETA = (8 * DEPTH) ** -0.25
NEG_INF = -1e30

kernel_name = 'bidir_hybrid_flow_backbone_step'


def rms_norm(x, g, eps=1e-6):
    xf = x.astype(jnp.float32)
    y = xf * lax.rsqrt(jnp.mean(xf * xf, axis=-1, keepdims=True) + eps)
    return (y * g.astype(jnp.float32)).astype(x.dtype)


def layer_norm(x, g, b, eps=1e-5):
    xf = x.astype(jnp.float32)
    mu = jnp.mean(xf, axis=-1, keepdims=True)
    var = jnp.mean(jnp.square(xf - mu), axis=-1, keepdims=True)
    return ((xf - mu) * lax.rsqrt(var + eps) * g.astype(jnp.float32) + b.astype(jnp.float32)).astype(x.dtype)


def l2_normalize(x, eps=1e-6):
    xf = x.astype(jnp.float32)
    return (xf * lax.rsqrt(jnp.sum(xf * xf, axis=-1, keepdims=True) + eps)).astype(x.dtype)


def axial_rope(x):
    T, dh = x.shape[1], x.shape[-1]
    t = jnp.arange(T)
    quarter = dh // 4
    inv_freq = ROPE_THETA ** (-jnp.arange(quarter, dtype=jnp.float32) / quarter)
    ang = jnp.concatenate([(t // GRID_W)[:, None] * inv_freq, (t % GRID_W)[:, None] * inv_freq], axis=-1)
    cos, sin = jnp.cos(ang)[None, :, None, :], jnp.sin(ang)[None, :, None, :]
    x1, x2 = x[..., : dh // 2], x[..., dh // 2:]
    return jnp.concatenate([x1 * cos - x2 * sin, x2 * cos + x1 * sin], axis=-1).astype(x.dtype)


def block_attention(q, k, v):
    B, T, H, dh = q.shape
    kvh = k.shape[2]
    qb = q.reshape(B, T // Q_BLOCK, Q_BLOCK, kvh, H // kvh, dh).swapaxes(0, 1)
    scale = dh ** -0.5

    def one_block(qi):
        s = jnp.einsum('bqkgd,blkd->bkgql', qi, k).astype(jnp.float32) * scale
        p = jax.nn.softmax(s, axis=-1).astype(v.dtype)
        return jnp.einsum('bkgql,blkd->bqkgd', p, v)

    o = lax.map(one_block, qb)
    return o.swapaxes(0, 1).reshape(B, T, H, dh)


def neighbourhood_attention(q, k, v, k_ctx, v_ctx, rpb):
    B, T, H, dh = q.shape
    R = T // GRID_W
    wr = min(WIN_R, R)
    rows = jnp.arange(R)
    cols = jnp.arange(GRID_W)
    row_idx = jnp.clip(rows - wr // 2, 0, R - wr)[:, None] + jnp.arange(wr)
    col_start = jnp.clip(cols - WIN_C // 2, 0, GRID_W - WIN_C)
    col_ok = (cols[None, :] >= col_start[:, None]) & (cols[None, :] < col_start[:, None] + WIN_C)
    dr = row_idx - rows[:, None] + (WIN_R - 1)
    dc = jnp.clip(cols[None, :] - cols[:, None], -(WIN_C - 1), WIN_C - 1) + (WIN_C - 1)
    bias = rpb[:, dr[:, None, :, None], dc[None, :, None, :]].astype(jnp.float32)
    bias = jnp.where(col_ok[None, None, :, None, :], bias, NEG_INF)
    qg = q.reshape(B, R, GRID_W, H, dh)
    kw = k.reshape(B, R, GRID_W, H, dh)[:, row_idx]
    vw = v.reshape(B, R, GRID_W, H, dh)[:, row_idx]
    scale = dh ** -0.5
    s_loc = jnp.einsum('brqhd,brakhd->bhrqak', qg, kw).astype(jnp.float32) * scale + bias[None]
    s_ctx = jnp.einsum('brqhd,blhd->bhrql', qg, k_ctx).astype(jnp.float32) * scale
    n_loc = wr * GRID_W
    s = jnp.concatenate([s_loc.reshape(B, H, R, GRID_W, n_loc), s_ctx], axis=-1)
    p = jax.nn.softmax(s, axis=-1).astype(v.dtype)
    p_loc = p[..., :n_loc].reshape(B, H, R, GRID_W, wr, GRID_W)
    o = jnp.einsum('bhrqak,brakhd->brqhd', p_loc, vw) + jnp.einsum('bhrql,blhd->brqhd', p[..., n_loc:], v_ctx)
    return o.reshape(B, T, H, dh)


def centred_depthwise_conv(x, w):
    pad = w.shape[0] // 2
    return lax.conv_general_dilated(x, w[:, None, :], window_strides=(1,), padding=[(pad, pad)],
                                    dimension_numbers=('NWC', 'WIO', 'NWC'), feature_group_count=x.shape[-1])


def to_chunks(a, size):
    B, T, H = a.shape[:3]
    a = a.reshape(B, T // size, size, H, *a.shape[3:])
    return jnp.moveaxis(a, (1, 2), (0, 3))


def from_chunks(a):
    a = jnp.moveaxis(a, (0, 3), (1, 2))
    return a.reshape(a.shape[0], a.shape[1] * a.shape[2], a.shape[3], a.shape[4])


def gdn_chunked(q, k, v, g, beta, s0):
    dt = q.dtype
    xs = [to_chunks(a.astype(jnp.float32), GDN_CHUNK) for a in (q, k, v, g, beta)]
    causal = jnp.tril(jnp.ones((GDN_CHUNK, GDN_CHUNK), bool))
    strict = jnp.tril(jnp.ones((GDN_CHUNK, GDN_CHUNK), bool), -1)
    eye = jnp.eye(GDN_CHUNK, dtype=jnp.float32)
    scale = DK_B ** -0.5

    def step(S, xc):
        qi, ki, vi, gi, bi = xc
        gc = jnp.cumsum(gi, axis=-1)
        decay = jnp.exp(jnp.where(causal, gc[..., :, None] - gc[..., None, :], NEG_INF))
        kb = ki * bi[..., None]
        a_mat = jnp.where(strict, jnp.einsum('bhid,bhjd->bhij', kb, ki) * decay, 0.0)
        rhs = jnp.concatenate([vi * bi[..., None], kb * jnp.exp(gc)[..., None]], axis=-1)
        sol = lax.linalg.triangular_solve(eye + a_mat, rhs, left_side=True, lower=True, unit_diagonal=True)
        u, w = sol[..., :DV_B], sol[..., DV_B:]
        v_new = u - jnp.einsum('bhcd,bhde->bhce', w, S)
        qs = qi * scale
        attn = jnp.einsum('bhid,bhjd->bhij', qs, ki) * decay
        o = jnp.einsum('bhcd,bhde->bhce', qs * jnp.exp(gc)[..., None], S) + jnp.einsum('bhij,bhje->bhie', attn, v_new)
        g_last = gc[..., -1]
        S = S * jnp.exp(g_last)[..., None, None] + jnp.einsum(
            'bhcd,bhce->bhde', ki * jnp.exp(g_last[..., None] - gc)[..., None], v_new)
        return S, o

    S, o = lax.scan(step, s0.astype(jnp.float32), tuple(xs))
    return from_chunks(o).astype(dt), S.astype(dt)


def mlstm_chunked(q, k, v, ig, lf, c0, n0, m0):
    dt = q.dtype
    xs = [to_chunks(a.astype(jnp.float32), MLSTM_CHUNK) for a in (q, k, v, ig, lf)]
    causal = jnp.tril(jnp.ones((MLSTM_CHUNK, MLSTM_CHUNK), bool))
    scale = DK_D ** -0.5

    def step(carry, xc):
        c, n, m = carry
        qi, ki, vi, igi, lfi = xc
        qi = qi * scale
        b = jnp.cumsum(lfi, axis=-1)
        a_inter = b + m[..., None]
        d_intra = jnp.where(causal, b[..., :, None] - b[..., None, :] + igi[..., None, :], NEG_INF)
        m_t = jnp.maximum(a_inter, jnp.max(d_intra, axis=-1))
        w_intra = jnp.exp(d_intra - m_t[..., None])
        w_inter = jnp.exp(a_inter - m_t)
        s = jnp.einsum('bhid,bhjd->bhij', qi, ki) * w_intra
        num = w_inter[..., None] * jnp.einsum('bhid,bhde->bhie', qi, c) + jnp.einsum('bhij,bhje->bhie', s, vi)
        den = w_inter * jnp.einsum('bhid,bhd->bhi', qi, n) + jnp.sum(s, axis=-1)
        h = num / jnp.maximum(jnp.abs(den), jnp.exp(-m_t))[..., None]
        b_last = b[..., -1]
        d_state = b_last[..., None] - b + igi
        m_new = jnp.maximum(b_last + m, jnp.max(d_state, axis=-1))
        w_old = jnp.exp(b_last + m - m_new)
        w_new = jnp.exp(d_state - m_new[..., None])
        c = w_old[..., None, None] * c + jnp.einsum('bhj,bhjd,bhje->bhde', w_new, ki, vi)
        n = w_old[..., None] * n + jnp.einsum('bhj,bhjd->bhd', w_new, ki)
        return (c, n, m_new), h

    (c, n, m), h = lax.scan(step, (c0.astype(jnp.float32), n0.astype(jnp.float32), m0.astype(jnp.float32)), tuple(xs))
    return from_chunks(h).astype(dt), c.astype(dt), n.astype(dt), m.astype(dt)


def rev(a):
    return jnp.flip(a, axis=1)


def qkv_a(h, w_in, q_norm, k_norm):
    B, T, _ = h.shape
    q, k, v = jnp.split(h @ w_in, [HEADS_A * HEAD_DIM_A, (HEADS_A + KV_HEADS_A) * HEAD_DIM_A], axis=-1)
    q = rms_norm(q.reshape(B, T, HEADS_A, HEAD_DIM_A), q_norm)
    k = rms_norm(k.reshape(B, T, KV_HEADS_A, HEAD_DIM_A), k_norm)
    return q, k, v.reshape(B, T, KV_HEADS_A, HEAD_DIM_A)


def mixer_a_context(h, w_in, q_norm, k_norm, w_out):
    q, k, v = qkv_a(h, w_in, q_norm, k_norm)
    return block_attention(q, k, v).reshape(h.shape) @ w_out, k, v


def mixer_a_latent(h, k_ctx, v_ctx, w_in, q_norm, k_norm, w_out):
    q, k, v = qkv_a(h, w_in, q_norm, k_norm)
    o = block_attention(axial_rope(q), jnp.concatenate([k_ctx, axial_rope(k)], axis=1),
                        jnp.concatenate([v_ctx, v], axis=1))
    return o.reshape(h.shape) @ w_out


def mixer_b(h, s0, w_in, w_conv, w_gate, a_log, dt_bias, o_norm, w_out):
    B, T, _ = h.shape
    proj = h @ w_in
    qkv = jax.nn.silu(centred_depthwise_conv(proj[..., :QKV_B], w_conv))
    q, k, v = jnp.split(qkv, [HEADS_B * DK_B, 2 * HEADS_B * DK_B], axis=-1)
    q = l2_normalize(q.reshape(B, T, HEADS_B, DK_B))
    k = l2_normalize(k.reshape(B, T, HEADS_B, DK_B))
    v = v.reshape(B, T, HEADS_B, DV_B)
    z = proj[..., QKV_B:].reshape(B, T, HEADS_B, DV_B)
    gates = (h @ w_gate).astype(jnp.float32).reshape(B, T, 2, 2, HEADS_B)
    g = -jnp.exp(a_log.astype(jnp.float32)) * jax.nn.softplus(gates[..., 0, :] + dt_bias.astype(jnp.float32))
    beta = jax.nn.sigmoid(gates[..., 1, :])
    o_f, s_f = gdn_chunked(q, k, v, g[:, :, 0], beta[:, :, 0], s0[:, 0])
    o_b, s_b = gdn_chunked(rev(q), rev(k), rev(v), rev(g[:, :, 1]), rev(beta[:, :, 1]), s0[:, 1])
    o = rms_norm(o_f + rev(o_b), o_norm) * jax.nn.silu(z)
    return o.reshape(h.shape) @ w_out, jnp.stack([s_f, s_b], axis=1)


def qkv_c(h, w_in):
    B, T, _ = h.shape
    q, k, v = jnp.split(h @ w_in, 3, axis=-1)
    return (q.reshape(B, T, HEADS_C, HEAD_DIM_C), k.reshape(B, T, HEADS_C, HEAD_DIM_C),
            v.reshape(B, T, HEADS_C, HEAD_DIM_C))


def mixer_c_context(h, w_in, w_out):
    q, k, v = qkv_c(h, w_in)
    return block_attention(q, k, v).reshape(h.shape) @ w_out, k, v


def mixer_c_latent(h, k_ctx, v_ctx, w_in, rpb, w_out):
    q, k, v = qkv_c(h, w_in)
    return neighbourhood_attention(q, k, v, k_ctx, v_ctx, rpb).reshape(h.shape) @ w_out


def mixer_d(h, c0, n0, m0, w_in, w_gate, b_gate, o_norm, w_out):
    B, T, _ = h.shape
    q, k, v, og = jnp.split(h @ w_in, [HEADS_D * DK_D, 2 * HEADS_D * DK_D, 2 * HEADS_D * DK_D + HEADS_D * DV_D], axis=-1)
    q = q.reshape(B, T, HEADS_D, DK_D)
    k = k.reshape(B, T, HEADS_D, DK_D)
    v = v.reshape(B, T, HEADS_D, DV_D)
    gates = (h @ w_gate).astype(jnp.float32).reshape(B, T, 2, 2, HEADS_D) + b_gate.astype(jnp.float32)
    ig = gates[..., 0, :]
    lf = jax.nn.log_sigmoid(gates[..., 1, :])
    h_f, c_f, n_f, m_f = mlstm_chunked(q, k, v, ig[:, :, 0], lf[:, :, 0], c0[:, 0], n0[:, 0], m0[:, 0])
    h_b, c_b, n_b, m_b = mlstm_chunked(rev(q), rev(k), rev(v), rev(ig[:, :, 1]), rev(lf[:, :, 1]),
                                       c0[:, 1], n0[:, 1], m0[:, 1])
    o = rms_norm(h_f + rev(h_b), o_norm) * jax.nn.sigmoid(og.reshape(B, T, HEADS_D, DV_D))
    return (o.reshape(h.shape) @ w_out, jnp.stack([c_f, c_b], axis=1), jnp.stack([n_f, n_b], axis=1),
            jnp.stack([m_f, m_b], axis=1))


def routed_experts(x, top_idx, gates, w_in, b_in, w_out, b_out):
    n_tok, d = x.shape
    n_asg = n_tok * TOP_K
    e_flat = top_idx.reshape(-1)
    tok_flat = jnp.arange(n_asg, dtype=jnp.int32) // TOP_K
    order = jnp.argsort(e_flat)
    e_sorted = e_flat[order]
    counts = jnp.zeros((N_EXPERTS,), jnp.int32).at[e_flat].add(1)
    padded = (counts + EXPERT_BLOCK - 1) // EXPERT_BLOCK * EXPERT_BLOCK
    pad_end = jnp.cumsum(padded)
    start = jnp.cumsum(counts) - counts
    slot = (pad_end - padded)[e_sorted] + jnp.arange(n_asg, dtype=jnp.int32) - start[e_sorted]
    n_blocks = -(-n_asg // EXPERT_BLOCK) + N_EXPERTS
    n_slots = n_blocks * EXPERT_BLOCK
    slot_tok = jnp.full((n_slots,), n_tok, jnp.int32).at[slot].set(tok_flat[order])
    slot_gate = jnp.zeros((n_slots,), gates.dtype).at[slot].set(gates.reshape(-1)[order])
    block_expert = jnp.minimum(
        jnp.searchsorted(pad_end, jnp.arange(n_blocks, dtype=jnp.int32) * EXPERT_BLOCK, side='right'), N_EXPERTS - 1)
    xb = jnp.concatenate([x, jnp.zeros((1, d), x.dtype)], axis=0)[slot_tok].reshape(n_blocks, EXPERT_BLOCK, d)

    def expert_block(args):
        xi, e = args
        glu, lin = jnp.split(xi @ w_in[e] + b_in[e], 2, axis=-1)
        glu = jnp.minimum(glu, SWIGLU_LIMIT)
        lin = jnp.clip(lin, -SWIGLU_LIMIT, SWIGLU_LIMIT)
        return (glu * jax.nn.sigmoid(SWIGLU_ALPHA * glu) * (lin + 1.0)) @ w_out[e] + b_out[e]

    yb = lax.map(expert_block, (xb, block_expert)).reshape(n_slots, d)
    y = jax.ops.segment_sum(yb * slot_gate[:, None], slot_tok, num_segments=n_tok + 1)
    return y[:n_tok].astype(x.dtype)


def moe_ffn(h, w_router, b_router, w_in, b_in, w_out, b_out):
    B, T, d = h.shape
    x = h.reshape(B * T, d)
    logits = (x @ w_router).astype(jnp.float32) + b_router.astype(jnp.float32)
    top_val, top_idx = lax.top_k(logits, TOP_K)
    gates = jax.nn.softmax(top_val, axis=-1)
    return routed_experts(x, top_idx, gates, w_in, b_in, w_out, b_out).reshape(B, T, d)


def modulation(cond, w_mod, b_mod):
    return jnp.split(jax.nn.silu(cond) @ w_mod + b_mod, 6, axis=-1)


def setup_inputs(seed: int = 0) -> dict:
    key = jax.random.key(seed)
    keys = iter(jax.random.split(key, 64))

    def nrm(shape, scale=1.0):
        return jax.random.normal(next(keys), shape, jnp.float32) * scale

    def gain(shape):
        return 1.0 + nrm(shape, 0.05)

    def unif(shape, lo, hi):
        return jax.random.uniform(next(keys), shape, jnp.float32, lo, hi)

    d_inv = D_MODEL ** -0.5
    out_scale = d_inv * DEEPNORM_BETA
    dt = jnp.exp(unif((N_B, 2, HEADS_B), float(np.log(1e-3)), float(np.log(1e-1))))
    return {
        'x_prompt': nrm((BATCH, SEQ, D_MODEL)),
        'x_sample': nrm((DEC_BATCH, DEC_SEQ, D_MODEL)),
        'cache_k_a': nrm((DEC_BATCH, N_A, PAST_LEN, KV_HEADS_A, HEAD_DIM_A)),
        'cache_v_a': nrm((DEC_BATCH, N_A, PAST_LEN, KV_HEADS_A, HEAD_DIM_A)),
        'state_b': nrm((DEC_BATCH, N_B, 2, HEADS_B, DK_B, DV_B), 0.1),
        'cache_k_c': nrm((DEC_BATCH, N_C, PAST_LEN, HEADS_C, HEAD_DIM_C)),
        'cache_v_c': nrm((DEC_BATCH, N_C, PAST_LEN, HEADS_C, HEAD_DIM_C)),
        'state_d_c': nrm((DEC_BATCH, N_D, 2, HEADS_D, DK_D, DV_D), 0.1),
        'state_d_n': nrm((DEC_BATCH, N_D, 2, HEADS_D, DK_D), 0.1),
        'state_d_m': nrm((DEC_BATCH, N_D, 2, HEADS_D)),
        'c': nrm((DEC_BATCH, D_MODEL)),
        'c_ctx': nrm((D_MODEL,)),
        'w_mod': nrm((DEPTH, D_MODEL, 6 * D_MODEL), 0.5 * d_inv),
        'b_mod': nrm((DEPTH, 6 * D_MODEL), 0.02),
        'ln_g': gain((DEPTH, 2, D_MODEL)),
        'ln_b': nrm((DEPTH, 2, D_MODEL), 0.02),
        'w_in_a': nrm((N_A, D_MODEL, (HEADS_A + 2 * KV_HEADS_A) * HEAD_DIM_A), d_inv),
        'q_norm_a': gain((N_A, HEAD_DIM_A)),
        'k_norm_a': gain((N_A, HEAD_DIM_A)),
        'w_out_a': nrm((N_A, D_MODEL, D_MODEL), out_scale),
        'w_in_b': nrm((N_B, D_MODEL, QKV_B + HEADS_B * DV_B), d_inv),
        'w_conv_b': nrm((N_B, CONV_W, QKV_B), CONV_W ** -0.5),
        'w_gate_b': nrm((N_B, D_MODEL, 4 * HEADS_B), d_inv),
        'a_log_b': jnp.log(unif((N_B, 2, HEADS_B), 1.0, 16.0)),
        'dt_bias_b': dt + jnp.log(-jnp.expm1(-dt)),
        'o_norm_b': gain((N_B, DV_B)),
        'w_out_b': nrm((N_B, D_MODEL, D_MODEL), out_scale),
        'w_in_c': nrm((N_C, D_MODEL, 3 * HEADS_C * HEAD_DIM_C), d_inv),
        'rpb_c': nrm((N_C, HEADS_C, 2 * WIN_R - 1, 2 * WIN_C - 1), 0.1),
        'w_out_c': nrm((N_C, D_MODEL, D_MODEL), out_scale),
        'w_in_d': nrm((N_D, D_MODEL, 2 * HEADS_D * DK_D + HEADS_D * DV_D + D_MODEL), d_inv),
        'w_gate_d': nrm((N_D, D_MODEL, 4 * HEADS_D), d_inv),
        'b_gate_d': jnp.concatenate([nrm((N_D, 2, 1, HEADS_D), 0.1), unif((N_D, 2, 1, HEADS_D), 3.0, 6.0)], axis=2),
        'o_norm_d': gain((N_D, DV_D)),
        'w_out_d': nrm((N_D, D_MODEL, D_MODEL), out_scale),
        'w_router': nrm((DEPTH, D_MODEL, N_EXPERTS), d_inv),
        'b_router': nrm((DEPTH, N_EXPERTS), 0.01),
        'w_exp_in': nrm((DEPTH, N_EXPERTS, D_MODEL, 2 * D_FF), d_inv),
        'b_exp_in': nrm((DEPTH, N_EXPERTS, 2 * D_FF), 0.02),
        'w_exp_out': nrm((DEPTH, N_EXPERTS, D_FF, D_MODEL), D_FF ** -0.5 * DEEPNORM_BETA),
        'b_exp_out': nrm((DEPTH, N_EXPERTS, D_MODEL), 0.02),
    }


def reference(x_prompt, x_sample, cache_k_a, cache_v_a, state_b, cache_k_c, cache_v_c, state_d_c, state_d_n,
              state_d_m, c, c_ctx, w_mod, b_mod, ln_g, ln_b, w_in_a, q_norm_a, k_norm_a, w_out_a, w_in_b, w_conv_b,
              w_gate_b, a_log_b, dt_bias_b, o_norm_b, w_out_b, w_in_c, rpb_c, w_out_c, w_in_d, w_gate_d, b_gate_d,
              o_norm_d, w_out_d, w_router, b_router, w_exp_in, b_exp_in, w_exp_out, b_exp_out):
    xp, xs = x_prompt, x_sample
    bp = xp.shape[0]
    new_k_a, new_v_a, new_s_b = [], [], []
    new_k_c, new_v_c = [], []
    new_c_d, new_n_d, new_m_d = [], [], []
    for i in range(DEPTH):
        kind, j = i % N_MIXERS, i // N_MIXERS
        sh1_p, sc1_p, g1_p, sh2_p, sc2_p, g2_p = modulation(c_ctx, w_mod[i], b_mod[i])
        sh1_s, sc1_s, g1_s, sh2_s, sc2_s, g2_s = [m[:, None, :] for m in modulation(c, w_mod[i], b_mod[i])]
        hp = xp * (1.0 + sc1_p) + sh1_p
        hs = xs * (1.0 + sc1_s) + sh1_s
        if kind == 0:
            yp, kp, vp = mixer_a_context(hp, w_in_a[j], q_norm_a[j], k_norm_a[j], w_out_a[j])
            ys = mixer_a_latent(hs, cache_k_a[:, j], cache_v_a[:, j], w_in_a[j], q_norm_a[j], k_norm_a[j], w_out_a[j])
            new_k_a.append(kp)
            new_v_a.append(vp)
        elif kind == 1:
            b_args = (w_in_b[j], w_conv_b[j], w_gate_b[j], a_log_b[j], dt_bias_b[j], o_norm_b[j], w_out_b[j])
            yp, sp = mixer_b(hp, jnp.zeros((bp, 2, HEADS_B, DK_B, DV_B), xp.dtype), *b_args)
            ys, _ = mixer_b(hs, state_b[:, j], *b_args)
            new_s_b.append(sp)
        elif kind == 2:
            yp, kp, vp = mixer_c_context(hp, w_in_c[j], w_out_c[j])
            ys = mixer_c_latent(hs, cache_k_c[:, j], cache_v_c[:, j], w_in_c[j], rpb_c[j], w_out_c[j])
            new_k_c.append(kp)
            new_v_c.append(vp)
        else:
            d_args = (w_in_d[j], w_gate_d[j], b_gate_d[j], o_norm_d[j], w_out_d[j])
            yp, cp, np_, mp = mixer_d(hp, jnp.zeros((bp, 2, HEADS_D, DK_D, DV_D), xp.dtype),
                                      jnp.zeros((bp, 2, HEADS_D, DK_D), xp.dtype),
                                      jnp.zeros((bp, 2, HEADS_D), xp.dtype), *d_args)
            ys, _, _, _ = mixer_d(hs, state_d_c[:, j], state_d_n[:, j], state_d_m[:, j], *d_args)
            new_c_d.append(cp)
            new_n_d.append(np_)
            new_m_d.append(mp)
        xp = layer_norm(DEEPNORM_ALPHA * xp + g1_p * yp, ln_g[i, 0], ln_b[i, 0])
        xs = layer_norm(DEEPNORM_ALPHA * xs + g1_s * ys, ln_g[i, 0], ln_b[i, 0])
        moe_args = (w_router[i], b_router[i], w_exp_in[i], b_exp_in[i], w_exp_out[i], b_exp_out[i])
        xp = layer_norm(DEEPNORM_ALPHA * xp + g2_p * moe_ffn(xp * (1.0 + sc2_p) + sh2_p, *moe_args),
                        ln_g[i, 1], ln_b[i, 1])
        xs = layer_norm(DEEPNORM_ALPHA * xs + g2_s * moe_ffn(xs * (1.0 + sc2_s) + sh2_s, *moe_args),
                        ln_g[i, 1], ln_b[i, 1])
    return (xp, xs, jnp.stack(new_k_a, axis=1), jnp.stack(new_v_a, axis=1), jnp.stack(new_s_b, axis=1),
            jnp.stack(new_k_c, axis=1), jnp.stack(new_v_c, axis=1), jnp.stack(new_c_d, axis=1),
            jnp.stack(new_n_d, axis=1), jnp.stack(new_m_d, axis=1))
```

```python
import functools

import jax
import jax.numpy as jnp
import numpy as np
from jax import lax
from jax.experimental import pallas as pl
from jax.experimental.pallas import tpu as pltpu

F32 = jnp.float32
BF16 = jnp.bfloat16

D = 1024
GRID_W = 64
HEADS_A, KV_A, DH_A = 16, 4, 64
ROPE_THETA = 10000.0
HEADS_B, DK_B = 8, 128
CONV_W = 3
CHUNK = 64
HEADS_C, DH_C = 16, 64
WIN_R, WIN_C = 8, 16
HEADS_D, DK_D, DV_D = 8, 64, 128
N_EXPERTS, TOP_K, D_FF = 32, 4, 1024
SWIGLU_LIMIT, SWIGLU_ALPHA = 7.0, 1.702
NEG_INF = -1e30
N_COND = 8
EXPERT_ROWS = 256
COMBINE_TOKENS = 64
VMEM_LIMIT = 56 * 1024 * 1024
_ALPHA = float((2 * 4) ** 0.25)


def _cparams(n_axes):
    return pltpu.CompilerParams(dimension_semantics=("arbitrary",) * n_axes, vmem_limit_bytes=VMEM_LIMIT)


def _dot(a, b):
    return jnp.dot(a.astype(BF16), b.astype(BF16), preferred_element_type=F32)


def _dot_nt(a, b):
    return lax.dot_general(a.astype(BF16), b.astype(BF16), (((1,), (1,)), ((), ())), preferred_element_type=F32)


def _dot_tn(a, b):
    return lax.dot_general(a.astype(BF16), b.astype(BF16), (((0,), (0,)), ((), ())), preferred_element_type=F32)


def _split2(x):
    hi = x.astype(BF16)
    lo = (x - hi.astype(F32)).astype(BF16)
    return hi, lo


def _split3(x):
    hi = x.astype(BF16)
    r = x - hi.astype(F32)
    mid = r.astype(BF16)
    lo = (r - mid.astype(F32)).astype(BF16)
    return hi, mid, lo


def _dot_hi(a, b):
    ah, al = _split2(a)
    bh, bl = _split2(b)
    d = functools.partial(jnp.dot, preferred_element_type=F32)
    return d(ah, bh) + (d(ah, bl) + d(al, bh))


def _mask_dot(mask_bf16, x):
    d = functools.partial(jnp.dot, preferred_element_type=F32)
    hi, mid, lo = _split3(x)
    return d(mask_bf16, hi) + (d(mask_bf16, mid) + d(mask_bf16, lo))


def _dot_mask(x, mask_bf16):
    d = functools.partial(jnp.dot, preferred_element_type=F32)
    hi, mid, lo = _split3(x)
    return d(hi, mask_bf16) + (d(mid, mask_bf16) + d(lo, mask_bf16))


def _layer_norm(x, g, b):
    mu = jnp.mean(x, axis=-1, keepdims=True)
    xc = x - mu
    var = jnp.mean(xc * xc, axis=-1, keepdims=True)
    return xc * lax.rsqrt(var + 1e-5) * g + b


def _sigmoid(x):
    return 1.0 / (1.0 + jnp.exp(-x))


def _cond_index(i, tm, n_prompt, t_lat):
    npt = n_prompt // tm
    return jnp.where(i < npt, 0, 1 + (i - npt) // (t_lat // tm))


def _mod_kernel(c_ref, w_ref, b_ref, o_ref):
    c = c_ref[...]
    o_ref[...] = _dot(c * _sigmoid(c), w_ref[...]) + b_ref[...]


def _modulation_table(cond, w, b):
    n = w.shape[1]
    tn = 1024
    return pl.pallas_call(
        _mod_kernel,
        out_shape=jax.ShapeDtypeStruct((N_COND, n), F32),
        grid=(n // tn,),
        in_specs=[pl.BlockSpec((N_COND, D), lambda j: (0, 0)),
                  pl.BlockSpec((D, tn), lambda j: (0, j)),
                  pl.BlockSpec((1, tn), lambda j: (0, j))],
        out_specs=pl.BlockSpec((N_COND, tn), lambda j: (0, j)),
        compiler_params=_cparams(1),
        name="mod_table",
    )(cond, w, b.reshape(1, n))


def _modmm_kernel(x_ref, sc_ref, sh_ref, w_ref, o_ref, wbf_ref):
    @pl.when(pl.program_id(1) == 0)
    def _():
        wbf_ref[...] = w_ref[...].astype(BF16)

    h = x_ref[...] * (1.0 + sc_ref[0]) + sh_ref[0]
    o_ref[...] = jnp.dot(h.astype(BF16), wbf_ref[...], preferred_element_type=F32)


def _col_tile(n):
    for t in (1024, 512, 256, 128):
        if n % t == 0:
            return t
    return n


def _modmm(x, mod, which_sc, which_sh, w, n_prompt, t_lat, tm=512):
    m = x.shape[0]
    n = w.shape[1]
    tn = _col_tile(n)
    cidx = functools.partial(_cond_index, tm=tm, n_prompt=n_prompt, t_lat=t_lat)
    return pl.pallas_call(
        _modmm_kernel,
        out_shape=jax.ShapeDtypeStruct((m, n), F32),
        grid=(n // tn, m // tm),
        in_specs=[pl.BlockSpec((tm, D), lambda j, i: (i, 0)),
                  pl.BlockSpec((1, 1, D), lambda j, i: (cidx(i), 0, which_sc)),
                  pl.BlockSpec((1, 1, D), lambda j, i: (cidx(i), 0, which_sh)),
                  pl.BlockSpec((D, tn), lambda j, i: (0, j))],
        out_specs=pl.BlockSpec((tm, tn), lambda j, i: (i, j)),
        scratch_shapes=[pltpu.VMEM((D, tn), BF16)],
        compiler_params=_cparams(2),
        name="mod_proj",
    )(x, mod, mod, w)


def _proj_kernel(*refs, kind):
    if kind == "plain":
        a_ref, w_ref, x_ref, g_ref, lg_ref, lb_ref, o_ref, wbf_ref = refs
    else:
        s_ref, z_ref, on_ref, w_ref, x_ref, g_ref, lg_ref, lb_ref, o_ref, wbf_ref, abuf_ref = refs

    @pl.when(pl.program_id(0) == 0)
    def _():
        wbf_ref[...] = w_ref[...].astype(BF16)

    if kind == "plain":
        a = a_ref[...].astype(BF16)
    else:
        s = s_ref[0] + s_ref[1]
        on = on_ref[...]
        hd = on.shape[-1]
        for h in range(D // hd):
            sl = slice(h * hd, (h + 1) * hd)
            sh = s[:, sl]
            y = sh * lax.rsqrt(jnp.mean(sh * sh, axis=-1, keepdims=True) + 1e-6) * on
            z = z_ref[:, sl]
            gate = _sigmoid(z)
            if kind == "silu":
                gate = z * gate
            abuf_ref[:, sl] = (y * gate).astype(BF16)
        a = abuf_ref[...]
    y = jnp.dot(a, wbf_ref[...], preferred_element_type=F32)
    o_ref[...] = _layer_norm(_ALPHA * x_ref[...] + g_ref[0] * y, lg_ref[...], lb_ref[...])


def _proj_res_ln(kind, a_args, w, x, mod, which_g, ln_g, ln_b, n_prompt, t_lat, z_colblock=0, tm=256):
    m = x.shape[0]
    cidx = functools.partial(_cond_index, tm=tm, n_prompt=n_prompt, t_lat=t_lat)
    row = pl.BlockSpec((tm, D), lambda i: (i, 0))
    vec = pl.BlockSpec((1, D), lambda i: (0, 0))
    if kind == "plain":
        a_specs = [row]
        scratch = [pltpu.VMEM((D, D), BF16)]
    else:
        on = a_args[2]
        a_specs = [pl.BlockSpec((2, tm, D), lambda i: (0, i, 0)),
                   pl.BlockSpec((tm, D), lambda i: (i, z_colblock)),
                   pl.BlockSpec((1, on.shape[-1]), lambda i: (0, 0))]
        scratch = [pltpu.VMEM((D, D), BF16), pltpu.VMEM((tm, D), BF16)]
    return pl.pallas_call(
        functools.partial(_proj_kernel, kind=kind),
        out_shape=jax.ShapeDtypeStruct((m, D), F32),
        grid=(m // tm,),
        in_specs=a_specs + [pl.BlockSpec((D, D), lambda i: (0, 0)), row,
                            pl.BlockSpec((1, 1, D), lambda i: (cidx(i), 0, which_g)), vec, vec],
        out_specs=row,
        scratch_shapes=scratch,
        compiler_params=_cparams(1),
        name="out_proj_" + kind,
    )(*a_args, w, x, mod, ln_g.reshape(1, D), ln_b.reshape(1, D))


def _prep_a_kernel(p_ref, qn_ref, kn_ref, cos_ref, sin_ref, q_ref, k_ref):
    cos = cos_ref[...]
    sin = sin_ref[...]
    half = DH_A // 2

    def one(x, g):
        y = x * lax.rsqrt(jnp.mean(x * x, axis=-1, keepdims=True) + 1e-6) * g
        swapped = jnp.concatenate([y[:, half:], y[:, :half]], axis=-1)
        return y * cos + swapped * sin

    for h in range(HEADS_A):
        sl = slice(h * DH_A, (h + 1) * DH_A)
        q_ref[:, sl] = one(p_ref[:, sl], qn_ref[...])
    for h in range(KV_A):
        sl = slice(h * DH_A, (h + 1) * DH_A)
        k_ref[:, sl] = one(p_ref[:, HEADS_A * DH_A + h * DH_A: HEADS_A * DH_A + (h + 1) * DH_A], kn_ref[...])


def _rope_tables(t_lat, tm):
    quarter = DH_A // 4
    t = np.arange(t_lat)
    inv_freq = ROPE_THETA ** (-np.arange(quarter, dtype=np.float32) / quarter)
    ang = np.concatenate([(t // GRID_W)[:, None] * inv_freq, (t % GRID_W)[:, None] * inv_freq], axis=-1)
    ang = jnp.asarray(ang, F32)
    cos, sin = jnp.cos(ang), jnp.sin(ang)
    cos64 = jnp.concatenate([cos, cos], axis=-1)
    sin64 = jnp.concatenate([-sin, sin], axis=-1)
    cos64 = jnp.concatenate([jnp.ones((tm, DH_A), F32), cos64], axis=0)
    sin64 = jnp.concatenate([jnp.zeros((tm, DH_A), F32), sin64], axis=0)
    return cos64, sin64


def _prep_a(p, q_norm, k_norm, n_prompt, t_lat, tm=256):
    m = p.shape[0]
    cos64, sin64 = _rope_tables(t_lat, tm)
    npt = n_prompt // tm

    def tab(i):
        return (jnp.where(i < npt, 0, 1 + (i - npt) % (t_lat // tm)), 0)

    return pl.pallas_call(
        _prep_a_kernel,
        out_shape=(jax.ShapeDtypeStruct((m, HEADS_A * DH_A), F32), jax.ShapeDtypeStruct((m, KV_A * DH_A), F32)),
        grid=(m // tm,),
        in_specs=[pl.BlockSpec((tm, p.shape[1]), lambda i: (i, 0)),
                  pl.BlockSpec((1, DH_A), lambda i: (0, 0)),
                  pl.BlockSpec((1, DH_A), lambda i: (0, 0)),
                  pl.BlockSpec((tm, DH_A), tab),
                  pl.BlockSpec((tm, DH_A), tab)],
        out_specs=(pl.BlockSpec((tm, HEADS_A * DH_A), lambda i: (i, 0)),
                   pl.BlockSpec((tm, KV_A * DH_A), lambda i: (i, 0))),
        compiler_params=_cparams(1),
        name="prep_a",
    )(p, q_norm.reshape(1, DH_A), k_norm.reshape(1, DH_A), cos64, sin64)


def _softmax_pv(scores, values):
    m = scores[0].max(axis=-1, keepdims=True)
    for s in scores[1:]:
        m = jnp.maximum(m, s.max(axis=-1, keepdims=True))
    ps = [jnp.exp(s - m) for s in scores]
    den = ps[0].sum(axis=-1, keepdims=True)
    for p in ps[1:]:
        den = den + p.sum(axis=-1, keepdims=True)
    inv = 1.0 / den
    o = _dot(ps[0] * inv, values[0])
    for p, v in zip(ps[1:], values[1:]):
        o = o + _dot(p * inv, v)
    return o


def _attn_kernel(*refs, heads, kv_heads, dh, nseg):
    q_ref = refs[0]
    kv = refs[1:1 + 2 * nseg]
    o_ref = refs[1 + 2 * nseg]
    scale = dh ** -0.5
    group = heads // kv_heads
    for g in range(kv_heads):
        ksl = slice(g * dh, (g + 1) * dh)
        ks = [kv[2 * s][:, ksl] for s in range(nseg)]
        vs = [kv[2 * s + 1][:, ksl] for s in range(nseg)]
        for j in range(group):
            h = g * group + j
            sl = slice(h * dh, (h + 1) * dh)
            q = q_ref[:, sl]
            scores = [_dot_nt(q, k) * scale for k in ks]
            o_ref[:, sl] = _softmax_pv(scores, vs)


def _attention(q_arr, q_spec, segs, out_rows, grid, heads, kv_heads, dh, out_spec):
    in_specs = [q_spec]
    args = [q_arr]
    for (k_arr, k_spec, v_arr, v_spec) in segs:
        in_specs += [k_spec, v_spec]
        args += [k_arr, v_arr]
    return pl.pallas_call(
        functools.partial(_attn_kernel, heads=heads, kv_heads=kv_heads, dh=dh, nseg=len(segs)),
        out_shape=jax.ShapeDtypeStruct((out_rows, heads * dh), F32),
        grid=grid,
        in_specs=in_specs,
        out_specs=out_spec,
        compiler_params=_cparams(len(grid)),
        name="attention",
    )(*args)


def _window_start(r, n_rows):
    wr = min(WIN_R, n_rows)
    return jnp.minimum(jnp.maximum(r - wr // 2, 0), n_rows - wr)


def _nbr_kernel(q_ref, kl_ref, vl_ref, kc_ref, vc_ref, bias_ref, o_ref, *, n_rows):
    r = pl.program_id(1)
    wr = min(WIN_R, n_rows)
    start = pl.multiple_of(_window_start(r, n_rows) * GRID_W, GRID_W)
    scale = DH_C ** -0.5
    for h in range(HEADS_C):
        sl = slice(h * DH_C, (h + 1) * DH_C)
        q = q_ref[:, sl]
        k_loc = kl_ref[pl.ds(start, wr * GRID_W), sl]
        v_loc = vl_ref[pl.ds(start, wr * GRID_W), sl]
        s_loc = _dot_nt(q, k_loc) * scale + bias_ref[0, h]
        s_ctx = _dot_nt(q, kc_ref[:, sl]) * scale
        o_ref[:, sl] = _softmax_pv([s_loc, s_ctx], [v_loc, vc_ref[:, sl]])


def _nbr_bias(rpb, n_rows):
    wr = min(WIN_R, n_rows)
    cols = np.arange(GRID_W)
    col_start = np.clip(cols - WIN_C // 2, 0, GRID_W - WIN_C)
    col_ok = (cols[None, :] >= col_start[:, None]) & (cols[None, :] < col_start[:, None] + WIN_C)
    dc = np.clip(cols[None, :] - cols[:, None], -(WIN_C - 1), WIN_C - 1) + (WIN_C - 1)
    n_off = 2 * WIN_R - wr
    dr = np.arange(n_off)[:, None] + np.arange(wr)[None, :]
    b = rpb[:, dr[:, None, :, None], dc[None, :, None, :]].astype(F32)
    b = jnp.where(col_ok[None, None, :, None, :], b, NEG_INF)
    return jnp.transpose(b, (1, 0, 2, 3, 4)).reshape(n_off, HEADS_C, GRID_W, wr * GRID_W)


def _nbr_attention(p, k_ctx, v_ctx, rpb, n_prompt, n_lat_b, t_lat, past):
    n_rows = t_lat // GRID_W
    wr = min(WIN_R, n_rows)
    bias = _nbr_bias(rpb, n_rows)
    row0 = n_prompt // GRID_W
    lat0 = n_prompt // t_lat
    hd = HEADS_C * DH_C

    def off(b, r):
        return (_window_start(r, n_rows) - r + (WIN_R - 1), 0, 0, 0)

    return pl.pallas_call(
        functools.partial(_nbr_kernel, n_rows=n_rows),
        out_shape=jax.ShapeDtypeStruct((n_lat_b * t_lat, hd), F32),
        grid=(n_lat_b, n_rows),
        in_specs=[pl.BlockSpec((GRID_W, hd), lambda b, r: (row0 + b * n_rows + r, 0)),
                  pl.BlockSpec((t_lat, hd), lambda b, r: (lat0 + b, 1)),
                  pl.BlockSpec((t_lat, hd), lambda b, r: (lat0 + b, 2)),
                  pl.BlockSpec((past, hd), lambda b, r: (b, 0)),
                  pl.BlockSpec((past, hd), lambda b, r: (b, 0)),
                  pl.BlockSpec((1, HEADS_C, GRID_W, wr * GRID_W), off)],
        out_specs=pl.BlockSpec((GRID_W, hd), lambda b, r: (b * n_rows + r, 0)),
        compiler_params=_cparams(2),
        name="nbr_attention",
    )(p, p, p, k_ctx, v_ctx, bias)


def _chunk_masks(d):
    sgn = 1 - 2 * d
    r = lax.broadcasted_iota(jnp.int32, (CHUNK, CHUNK), 0)
    c = lax.broadcasted_iota(jnp.int32, (CHUNK, CHUNK), 1)
    diff = (r - c) * sgn
    incl = diff >= 0
    strict = diff > 0
    incl_t = diff <= 0
    return incl, strict, incl_t


def _gate_kernel(raw_ref, p0_ref, p1_ref, o_ref, *, kind):
    raw = raw_ref[...]
    lane = lax.broadcasted_iota(jnp.int32, raw.shape, 1)
    first = (lane % 16) < 8
    if kind == "gdn":
        g = -jnp.exp(p0_ref[...]) * jax.nn.softplus(raw + p1_ref[...])
        o_ref[...] = jnp.where(first, g, _sigmoid(raw))
    else:
        x = raw + p0_ref[...]
        o_ref[...] = jnp.where(first, x, -jax.nn.softplus(-x))


def _gates(kind, raw, p0, p1, tm=512):
    m, n = raw.shape
    return pl.pallas_call(
        functools.partial(_gate_kernel, kind=kind),
        out_shape=jax.ShapeDtypeStruct((m, n), F32),
        grid=(m // tm,),
        in_specs=[pl.BlockSpec((tm, n), lambda i: (i, 0)),
                  pl.BlockSpec((1, n), lambda i: (0, 0)),
                  pl.BlockSpec((1, n), lambda i: (0, 0))],
        out_specs=pl.BlockSpec((tm, n), lambda i: (i, 0)),
        compiler_params=_cparams(1),
        name="gates_" + kind,
    )(raw, p0, p1)


def _gate_layouts(g):
    m = g.shape[0]
    col = g.reshape(m // CHUNK, CHUNK, 2, 16).transpose(2, 0, 1, 3)
    return col, col.transpose(0, 1, 3, 2)


def _prep_b_kernel(p_ref, prev_ref, next_ref, w_ref, q_ref, k_ref, v_ref, *, tm, tiles_per_seq):
    i = pl.program_id(0)
    pos = i % tiles_per_seq
    rows = lax.broadcasted_iota(jnp.int32, (tm, D), 0)
    outs = (q_ref, k_ref, v_ref)
    for seg in range(3):
        sl = slice(seg * D, (seg + 1) * D)
        x = p_ref[:, sl]
        prev_row = jnp.where(pos == 0, 0.0, prev_ref[7:8, sl])
        next_row = jnp.where(pos == tiles_per_seq - 1, 0.0, next_ref[0:1, sl])
        x_prev = jnp.where(rows == 0, prev_row, pltpu.roll(x, 1, axis=0))
        x_next = jnp.where(rows == tm - 1, next_row, pltpu.roll(x, tm - 1, axis=0))
        y = x_prev * w_ref[0:1, sl] + x * w_ref[1:2, sl] + x_next * w_ref[2:3, sl]
        y = y * _sigmoid(y)
        if seg < 2:
            for h in range(HEADS_B):
                hs = slice(h * DK_B, (h + 1) * DK_B)
                yh = y[:, hs]
                outs[seg][:, hs] = yh * lax.rsqrt(jnp.sum(yh * yh, axis=-1, keepdims=True) + 1e-6)
        else:
            v_ref[...] = y


def _prep_b(p, w_conv, row0, n_rows, t_seq, tm=256):
    total8 = p.shape[0] // 8
    r0 = row0 // tm
    tiles_per_seq = t_seq // tm
    out = jax.ShapeDtypeStruct((n_rows, D), F32)
    ospec = pl.BlockSpec((tm, D), lambda i: (i, 0))
    return pl.pallas_call(
        functools.partial(_prep_b_kernel, tm=tm, tiles_per_seq=tiles_per_seq),
        out_shape=(out, out, out),
        grid=(n_rows // tm,),
        in_specs=[pl.BlockSpec((tm, p.shape[1]), lambda i: (r0 + i, 0)),
                  pl.BlockSpec((8, p.shape[1]), lambda i: (jnp.maximum((r0 + i) * (tm // 8) - 1, 0), 0)),
                  pl.BlockSpec((8, p.shape[1]), lambda i: (jnp.minimum((r0 + i + 1) * (tm // 8), total8 - 1), 0)),
                  pl.BlockSpec((CONV_W, 3 * D), lambda i: (0, 0))],
        out_specs=(ospec, ospec, ospec),
        compiler_params=_cparams(1),
        name="prep_b",
    )(p, p, p, w_conv)


def _unit_tri_inverse(a):
    n = -a
    r = lax.broadcasted_iota(jnp.int32, a.shape, 0)
    c = lax.broadcasted_iota(jnp.int32, a.shape, 1)
    t = jnp.where(r == c, 1.0, 0.0) + n
    p = n
    steps = int(np.log2(CHUNK)) - 1
    for _ in range(steps):
        p = _dot_hi(p, p)
        t = t + _dot_hi(t, p)
    return t


def _gdn_kernel(*refs, n_chunks, zero_init):
    if zero_init:
        q_ref, k_ref, v_ref, gc_ref, gr_ref, o_ref, sfin_ref, s_ref = refs
    else:
        q_ref, k_ref, v_ref, gc_ref, gr_ref, s0_ref, o_ref, sfin_ref, s_ref = refs
    d = pl.program_id(1)
    c = pl.program_id(2)

    @pl.when(c == 0)
    def _():
        if zero_init:
            s_ref[...] = jnp.zeros(s_ref.shape, F32)
        else:
            s_ref[...] = s0_ref[0, 0]

    incl, strict, incl_t = _chunk_masks(d)
    m_incl = jnp.where(incl, 1.0, 0.0).astype(BF16)
    m_incl_t = jnp.where(incl_t, 1.0, 0.0).astype(BF16)
    gcol = gc_ref[0, 0]
    grow = gr_ref[0, 0]
    gcs = _mask_dot(m_incl, gcol)
    grs = _dot_mask(grow, m_incl_t)
    last = jnp.where(d == 0, CHUNK - 1, 0)
    rowi = lax.broadcasted_iota(jnp.int32, gcs.shape, 0)
    g_last = jnp.sum(jnp.where(rowi == last, gcs, 0.0), axis=0, keepdims=True)
    scale = DK_B ** -0.5
    for h in range(HEADS_B):
        hs = slice(h * DK_B, (h + 1) * DK_B)
        gc_c = gcs[:, h:h + 1]
        gc_r = grs[h:h + 1, :]
        b_c = gcol[:, HEADS_B + h:HEADS_B + h + 1]
        decay = jnp.exp(jnp.where(incl, gc_c - gc_r, NEG_INF))
        qh, kh, vh = q_ref[:, hs], k_ref[:, hs], v_ref[:, hs]
        kb = kh * b_c
        a_mat = jnp.where(strict, _dot_nt(kb, kh) * decay, 0.0)
        t_inv = _unit_tri_inverse(a_mat)
        egc = jnp.exp(gc_c)
        u = _dot_hi(t_inv, vh * b_c)
        w = _dot_hi(t_inv, kb * egc)
        s = s_ref[h]
        v_new = u - _dot(w, s)
        qs = qh * scale
        attn = _dot_nt(qs, kh) * decay
        o_ref[0, :, hs] = _dot(qs * egc, s) + _dot(attn, v_new)
        gl = g_last[:, h:h + 1]
        s_ref[h] = s * jnp.exp(gl) + _dot_tn(kh * jnp.exp(gl - gc_c), v_new)

    @pl.when(c == n_chunks - 1)
    def _():
        sfin_ref[0, 0] = s_ref[...]


def _chunk_index(d, c, n_chunks):
    return c + d * (n_chunks - 1 - 2 * c)


def _gdn(q, k, v, gcol, grow, s0, n_seq, t_seq, chunk0):
    n_chunks = t_seq // CHUNK

    def rows(b, d, c):
        return (b * n_chunks + _chunk_index(d, c, n_chunks), 0)

    def gidx(b, d, c):
        return (d, chunk0 + b * n_chunks + _chunk_index(d, c, n_chunks), 0, 0)

    blk = pl.BlockSpec((CHUNK, D), rows)
    in_specs = [blk, blk, blk,
                pl.BlockSpec((1, 1, CHUNK, 16), gidx),
                pl.BlockSpec((1, 1, 16, CHUNK), gidx)]
    args = [q, k, v, gcol, grow]
    state_spec = pl.BlockSpec((1, 1, HEADS_B, DK_B, DK_B), lambda b, d, c: (b, d, 0, 0, 0))
    if s0 is not None:
        in_specs.append(state_spec)
        args.append(s0)
    m = n_seq * t_seq
    return pl.pallas_call(
        functools.partial(_gdn_kernel, n_chunks=n_chunks, zero_init=s0 is None),
        out_shape=(jax.ShapeDtypeStruct((2, m, D), F32),
                   jax.ShapeDtypeStruct((n_seq, 2, HEADS_B, DK_B, DK_B), F32)),
        grid=(n_seq, 2, n_chunks),
        in_specs=in_specs,
        out_specs=(pl.BlockSpec((1, CHUNK, D), lambda b, d, c: (d,) + rows(b, d, c)), state_spec),
        scratch_shapes=[pltpu.VMEM((HEADS_B, DK_B, DK_B), F32)],
        compiler_params=_cparams(3),
        name="gdn",
    )(*args)


def _mlstm_kernel(*refs, n_chunks, zero_init):
    if zero_init:
        qk_ref, v_ref, gc_ref, gr_ref, o_ref, cfin_ref, nfin_ref, mfin_ref, c_ref, n_ref, m_ref = refs
    else:
        (qk_ref, v_ref, gc_ref, gr_ref, c0_ref, n0_ref, m0_ref,
         o_ref, cfin_ref, nfin_ref, mfin_ref, c_ref, n_ref, m_ref) = refs
    d = pl.program_id(1)
    c = pl.program_id(2)

    @pl.when(c == 0)
    def _():
        if zero_init:
            c_ref[...] = jnp.zeros(c_ref.shape, F32)
            n_ref[...] = jnp.zeros(n_ref.shape, F32)
            m_ref[...] = jnp.zeros(m_ref.shape, F32)
        else:
            c_ref[...] = c0_ref[0, 0]
            n_ref[...] = n0_ref[0, 0]
            m_ref[...] = m0_ref[0, 0]

    incl, _, incl_t = _chunk_masks(d)
    m_incl = jnp.where(incl, 1.0, 0.0).astype(BF16)
    m_incl_t = jnp.where(incl_t, 1.0, 0.0).astype(BF16)
    gcol = gc_ref[0, 0]
    grow = gr_ref[0, 0]
    bcs = _mask_dot(m_incl, gcol)
    brs = _dot_mask(grow, m_incl_t)
    last = jnp.where(d == 0, CHUNK - 1, 0)
    rowi = lax.broadcasted_iota(jnp.int32, bcs.shape, 0)
    b_last_all = jnp.sum(jnp.where(rowi == last, bcs, 0.0), axis=0, keepdims=True)
    scale = DK_D ** -0.5
    for h in range(HEADS_D):
        ks = slice(h * DK_D, (h + 1) * DK_D)
        vs = slice(h * DV_D, (h + 1) * DV_D)
        qh = qk_ref[:, ks] * scale
        kh = qk_ref[:, HEADS_D * DK_D + h * DK_D: HEADS_D * DK_D + (h + 1) * DK_D]
        vh = v_ref[:, vs]
        b_c = bcs[:, HEADS_D + h:HEADS_D + h + 1]
        b_r = brs[HEADS_D + h:HEADS_D + h + 1, :]
        ig_c = gcol[:, h:h + 1]
        ig_r = grow[h:h + 1, :]
        m_old = m_ref[h:h + 1, 0:1]
        c_old = c_ref[h]
        n_old = n_ref[h:h + 1, :]
        a_inter = b_c + m_old
        d_intra = jnp.where(incl, b_c - b_r + ig_r, NEG_INF)
        m_t = jnp.maximum(a_inter, jnp.max(d_intra, axis=-1, keepdims=True))
        w_intra = jnp.exp(d_intra - m_t)
        w_inter = jnp.exp(a_inter - m_t)
        s = _dot_nt(qh, kh) * w_intra
        num = w_inter * _dot(qh, c_old) + _dot(s, vh)
        qn = jnp.sum(qh * n_old, axis=-1, keepdims=True)
        den = w_inter * qn + jnp.sum(s, axis=-1, keepdims=True)
        o_ref[0, :, vs] = num / jnp.maximum(jnp.abs(den), jnp.exp(-m_t))
        b_last = b_last_all[:, HEADS_D + h:HEADS_D + h + 1]
        d_state = b_last - b_c + ig_c
        m_new = jnp.maximum(b_last + m_old, jnp.max(d_state, axis=0, keepdims=True))
        w_old = jnp.exp(b_last + m_old - m_new)
        w_new = jnp.exp(d_state - m_new)
        kw = kh * w_new
        c_ref[h] = w_old * c_old + _dot_tn(kw, vh)
        n_ref[h:h + 1, :] = w_old * n_old + jnp.sum(kw, axis=0, keepdims=True)
        m_ref[h:h + 1, :] = jnp.broadcast_to(m_new, (1, m_ref.shape[1]))

    @pl.when(c == n_chunks - 1)
    def _():
        cfin_ref[0, 0] = c_ref[...]
        nfin_ref[0, 0] = n_ref[...]
        mfin_ref[0, 0] = m_ref[...]


def _mlstm(p, gcol, grow, states, n_seq, t_seq, row0):
    n_chunks = t_seq // CHUNK
    chunk0 = row0 // CHUNK

    def rows(b, d, c):
        return chunk0 + b * n_chunks + _chunk_index(d, c, n_chunks)

    def gidx(b, d, c):
        return (d, rows(b, d, c), 0, 0)

    in_specs = [pl.BlockSpec((CHUNK, D), lambda b, d, c: (rows(b, d, c), 0)),
                pl.BlockSpec((CHUNK, D), lambda b, d, c: (rows(b, d, c), 1)),
                pl.BlockSpec((1, 1, CHUNK, 16), gidx),
                pl.BlockSpec((1, 1, 16, CHUNK), gidx)]
    args = [p, p, gcol, grow]
    c_spec = pl.BlockSpec((1, 1, HEADS_D, DK_D, DV_D), lambda b, d, c: (b, d, 0, 0, 0))
    n_spec = pl.BlockSpec((1, 1, HEADS_D, DK_D), lambda b, d, c: (b, d, 0, 0))
    m_spec = pl.BlockSpec((1, 1, HEADS_D, 128), lambda b, d, c: (b, d, 0, 0))
    if states is not None:
        in_specs += [c_spec, n_spec, m_spec]
        args += list(states)
    m = n_seq * t_seq
    return pl.pallas_call(
        functools.partial(_mlstm_kernel, n_chunks=n_chunks, zero_init=states is None),
        out_shape=(jax.ShapeDtypeStruct((2, m, D), F32),
                   jax.ShapeDtypeStruct((n_seq, 2, HEADS_D, DK_D, DV_D), F32),
                   jax.ShapeDtypeStruct((n_seq, 2, HEADS_D, DK_D), F32),
                   jax.ShapeDtypeStruct((n_seq, 2, HEADS_D, 128), F32)),
        grid=(n_seq, 2, n_chunks),
        in_specs=in_specs,
        out_specs=(pl.BlockSpec((1, CHUNK, D),
                                lambda b, d, c: (d, b * n_chunks + _chunk_index(d, c, n_chunks), 0)),
                   c_spec, n_spec, m_spec),
        scratch_shapes=[pltpu.VMEM((HEADS_D, DK_D, DV_D), F32),
                        pltpu.VMEM((HEADS_D, DK_D), F32),
                        pltpu.VMEM((HEADS_D, 128), F32)],
        compiler_params=_cparams(3),
        name="mlstm",
    )(*args)


def _router_kernel(x_ref, sc_ref, sh_ref, w_ref, b_ref, h_ref, idx_ref, gate_ref):
    h = x_ref[...] * (1.0 + sc_ref[0]) + sh_ref[0]
    h_ref[...] = h
    logits = _dot(h, w_ref[...]) + b_ref[...]
    lane = lax.broadcasted_iota(jnp.int32, logits.shape, 1)
    out_lane = lax.broadcasted_iota(jnp.int32, idx_ref.shape, 1)
    idx_out = jnp.zeros(idx_ref.shape, jnp.int32)
    val_out = jnp.zeros(gate_ref.shape, F32)
    v0 = None
    den = None
    for k in range(TOP_K):
        m = jnp.max(logits, axis=-1, keepdims=True)
        idx = jnp.min(jnp.where(logits == m, lane, N_EXPERTS), axis=-1, keepdims=True)
        logits = jnp.where(lane == idx, -jnp.inf, logits)
        if k == 0:
            v0 = m
        e = jnp.exp(m - v0)
        den = e if den is None else den + e
        idx_out = jnp.where(out_lane == k, idx, idx_out)
        val_out = jnp.where(out_lane == k, e, val_out)
    idx_ref[...] = idx_out
    gate_ref[...] = val_out / den


def _router(x, mod, w, b, n_prompt, t_lat, tm=256):
    m = x.shape[0]
    cidx = functools.partial(_cond_index, tm=tm, n_prompt=n_prompt, t_lat=t_lat)
    row = pl.BlockSpec((tm, D), lambda i: (i, 0))
    lanes = pl.BlockSpec((tm, 128), lambda i: (i, 0))
    return pl.pallas_call(
        _router_kernel,
        out_shape=(jax.ShapeDtypeStruct((m, D), F32),
                   jax.ShapeDtypeStruct((m, 128), jnp.int32),
                   jax.ShapeDtypeStruct((m, 128), F32)),
        grid=(m // tm,),
        in_specs=[row,
                  pl.BlockSpec((1, 1, D), lambda i: (cidx(i), 0, 4)),
                  pl.BlockSpec((1, 1, D), lambda i: (cidx(i), 0, 3)),
                  pl.BlockSpec((D, N_EXPERTS), lambda i: (0, 0)),
                  pl.BlockSpec((1, N_EXPERTS), lambda i: (0, 0))],
        out_specs=(row, lanes, lanes),
        compiler_params=_cparams(1),
        name="router",
    )(x, mod, mod, w, b.reshape(1, N_EXPERTS))


def _expert_kernel(be_ref, nb_ref, tok_ref, h_hbm, win_ref, bin_ref, wout_ref, bout_ref, o_ref,
                   xbuf, sem, winbf, woutbf):
    i = pl.program_id(0)
    nb = nb_ref[0]
    slot = i % 2

    def issue(block, s):
        def body(r, carry):
            t = tok_ref[block * EXPERT_ROWS + r]
            pltpu.make_async_copy(h_hbm.at[pl.ds(t, 1), :], xbuf.at[s, pl.ds(r, 1), :], sem.at[s]).start()
            return carry
        lax.fori_loop(0, EXPERT_ROWS, body, 0, unroll=8)

    @pl.when(i == 0)
    def _():
        issue(0, 0)

    @pl.when(i + 1 < nb)
    def _():
        issue(i + 1, 1 - slot)

    new_expert = jnp.logical_or(i == 0, be_ref[i] != be_ref[jnp.maximum(i - 1, 0)])

    @pl.when(jnp.logical_and(new_expert, i < nb))
    def _():
        winbf[...] = win_ref[0].astype(BF16)
        woutbf[...] = wout_ref[0].astype(BF16)

    @pl.when(i < nb)
    def _():
        pltpu.make_async_copy(h_hbm.at[pl.ds(0, EXPERT_ROWS), :], xbuf.at[slot], sem.at[slot]).wait()
        x = xbuf[slot].astype(BF16)
        hmid = jnp.dot(x, winbf[...], preferred_element_type=F32) + bin_ref[0]
        glu = jnp.minimum(hmid[:, :D_FF], SWIGLU_LIMIT)
        lin = jnp.clip(hmid[:, D_FF:], -SWIGLU_LIMIT, SWIGLU_LIMIT)
        act = glu * _sigmoid(SWIGLU_ALPHA * glu) * (lin + 1.0)
        o_ref[...] = jnp.dot(act.astype(BF16), woutbf[...], preferred_element_type=F32) + bout_ref[0]

    @pl.when(i >= nb)
    def _():
        o_ref[...] = jnp.zeros(o_ref.shape, F32)


def _experts(h, block_expert, n_used, slot_tok, w_in, b_in, w_out, b_out):
    n_blocks = block_expert.shape[0]
    grid_spec = pltpu.PrefetchScalarGridSpec(
        num_scalar_prefetch=3,
        grid=(n_blocks,),
        in_specs=[pl.BlockSpec(memory_space=pl.ANY),
                  pl.BlockSpec((1, D, 2 * D_FF), lambda i, be, nb, tok: (be[i], 0, 0)),
                  pl.BlockSpec((1, 1, 2 * D_FF), lambda i, be, nb, tok: (be[i], 0, 0)),
                  pl.BlockSpec((1, D_FF, D), lambda i, be, nb, tok: (be[i], 0, 0)),
                  pl.BlockSpec((1, 1, D), lambda i, be, nb, tok: (be[i], 0, 0))],
        out_specs=pl.BlockSpec((EXPERT_ROWS, D), lambda i, be, nb, tok: (i, 0)),
        scratch_shapes=[pltpu.VMEM((2, EXPERT_ROWS, D), F32),
                        pltpu.SemaphoreType.DMA((2,)),
                        pltpu.VMEM((D, 2 * D_FF), BF16),
                        pltpu.VMEM((D_FF, D), BF16)])
    return pl.pallas_call(
        _expert_kernel,
        out_shape=jax.ShapeDtypeStruct((n_blocks * EXPERT_ROWS, D), F32),
        grid_spec=grid_spec,
        compiler_params=_cparams(1),
        name="experts",
    )(block_expert, n_used, slot_tok, h, w_in, b_in.reshape(N_EXPERTS, 1, 2 * D_FF), w_out,
      b_out.reshape(N_EXPERTS, 1, D))


def _combine_kernel(slot_ref, y_hbm, gate_ref, x_ref, g_ref, lg_ref, lb_ref, o_ref, ybuf, sem):
    i = pl.program_id(0)
    n = pl.num_programs(0)
    tc = COMBINE_TOKENS
    slot = i % 2

    def issue(tile, s):
        def body(a, carry):
            src = slot_ref[tile * (tc * TOP_K) + a]
            pltpu.make_async_copy(y_hbm.at[pl.ds(src, 1), :], ybuf.at[s, a % TOP_K, pl.ds(a // TOP_K, 1), :],
                                  sem.at[s]).start()
            return carry
        lax.fori_loop(0, tc * TOP_K, body, 0, unroll=8)

    @pl.when(i == 0)
    def _():
        issue(0, 0)

    @pl.when(i + 1 < n)
    def _():
        issue(i + 1, 1 - slot)

    for k in range(TOP_K):
        pltpu.make_async_copy(y_hbm.at[pl.ds(0, tc), :], ybuf.at[slot, k], sem.at[slot]).wait()
    gates = gate_ref[...]
    y = gates[:, 0:1] * ybuf[slot, 0]
    for k in range(1, TOP_K):
        y = y + gates[:, k:k + 1] * ybuf[slot, k]
    o_ref[...] = _layer_norm(_ALPHA * x_ref[...] + g_ref[0] * y, lg_ref[...], lb_ref[...])


def _combine(slot_of, yb, gates, x, mod, ln_g, ln_b, n_prompt, t_lat):
    m = x.shape[0]
    tc = COMBINE_TOKENS
    cidx = functools.partial(_cond_index, tm=tc, n_prompt=n_prompt, t_lat=t_lat)
    row = pl.BlockSpec((tc, D), lambda i, s: (i, 0))
    vec = pl.BlockSpec((1, D), lambda i, s: (0, 0))
    grid_spec = pltpu.PrefetchScalarGridSpec(
        num_scalar_prefetch=1,
        grid=(m // tc,),
        in_specs=[pl.BlockSpec(memory_space=pl.ANY),
                  pl.BlockSpec((tc, 128), lambda i, s: (i, 0)),
                  row,
                  pl.BlockSpec((1, 1, D), lambda i, s: (cidx(i), 0, 5)),
                  vec, vec],
        out_specs=row,
        scratch_shapes=[pltpu.VMEM((2, TOP_K, tc, D), F32), pltpu.SemaphoreType.DMA((2,))])
    return pl.pallas_call(
        _combine_kernel,
        out_shape=jax.ShapeDtypeStruct((m, D), F32),
        grid_spec=grid_spec,
        compiler_params=_cparams(1),
        name="combine",
    )(slot_of, yb, gates, x, mod, ln_g.reshape(1, D), ln_b.reshape(1, D))


def _routing_tables(top_idx):
    n_tok = top_idx.shape[0]
    n_asg = n_tok * TOP_K
    e_flat = top_idx.reshape(-1)
    order = jnp.argsort(e_flat)
    e_sorted = e_flat[order]
    counts = jnp.zeros((N_EXPERTS,), jnp.int32).at[e_flat].add(1)
    padded = (counts + EXPERT_ROWS - 1) // EXPERT_ROWS * EXPERT_ROWS
    pad_end = jnp.cumsum(padded)
    start = jnp.cumsum(counts) - counts
    slot = (pad_end - padded)[e_sorted] + jnp.arange(n_asg, dtype=jnp.int32) - start[e_sorted]
    n_blocks = n_asg // EXPERT_ROWS + N_EXPERTS
    n_slots = n_blocks * EXPERT_ROWS
    slot_tok = jnp.zeros((n_slots,), jnp.int32).at[slot].set((order // TOP_K).astype(jnp.int32))
    slot_of = jnp.zeros((n_asg,), jnp.int32).at[order].set(slot.astype(jnp.int32))
    n_used = (pad_end[-1] // EXPERT_ROWS).astype(jnp.int32)
    blocks = jnp.arange(n_blocks, dtype=jnp.int32)
    block_expert = jnp.minimum(jnp.searchsorted(pad_end, blocks * EXPERT_ROWS, side="right"), N_EXPERTS - 1)
    last_used = block_expert[jnp.maximum(n_used - 1, 0)]
    block_expert = jnp.where(blocks < n_used, block_expert, last_used).astype(jnp.int32)
    return block_expert, n_used.reshape(1), slot_tok, slot_of


def _moe(x, mod, w_router, b_router, w_in, b_in, w_out, b_out, ln_g, ln_b, n_prompt, t_lat):
    h, idx, gates = _router(x, mod, w_router, b_router, n_prompt, t_lat)
    block_expert, n_used, slot_tok, slot_of = _routing_tables(idx[:, :TOP_K])
    yb = _experts(h, block_expert, n_used, slot_tok, w_in, b_in, w_out, b_out)
    return _combine(slot_of, yb, gates, x, mod, ln_g, ln_b, n_prompt, t_lat)


def kernel(x_prompt, x_sample, cache_k_a, cache_v_a, state_b, cache_k_c, cache_v_c, state_d_c, state_d_n, state_d_m, c, c_ctx, w_mod, b_mod, ln_g, ln_b, w_in_a, q_norm_a, k_norm_a, w_out_a, w_in_b, w_conv_b, w_gate_b, a_log_b, dt_bias_b, o_norm_b, w_out_b, w_in_c, rpb_c, w_out_c, w_in_d, w_gate_d, b_gate_d, o_norm_d, w_out_d, w_router, b_router, w_exp_in, b_exp_in, w_exp_out, b_exp_out):
    bp, t_p, _ = x_prompt.shape
    bs, t_s, _ = x_sample.shape
    past = cache_k_a.shape[2]
    depth = w_mod.shape[0]
    assert depth == 4 and bs + 1 <= N_COND
    n_p, n_s = bp * t_p, bs * t_s
    x = jnp.concatenate([x_prompt.reshape(n_p, D), x_sample.reshape(n_s, D)], axis=0)
    cond = jnp.concatenate([c_ctx.reshape(1, D), c, jnp.zeros((N_COND - 1 - bs, D), F32)], axis=0)
    res = functools.partial(_proj_res_ln, n_prompt=n_p, t_lat=t_s)
    outs = {}
    for i in range(depth):
        kind, j = i % 4, i // 4
        mod = _modulation_table(cond, w_mod[i], b_mod[i]).reshape(N_COND, 1, 6 * D)
        proj = functools.partial(_modmm, x, mod, 1, 0, n_prompt=n_p, t_lat=t_s)
        if kind == 0:
            p = proj(w_in_a[j])
            q, k = _prep_a(p, q_norm_a[j], k_norm_a[j], n_p, t_s)
            hq, hk = HEADS_A * DH_A, KV_A * DH_A
            vcol = hq // hk + 1
            o_p = _attention(q, pl.BlockSpec((t_p, hq), lambda b: (b, 0)),
                             [(k, pl.BlockSpec((t_p, hk), lambda b: (b, 0)),
                               p, pl.BlockSpec((t_p, hk), lambda b: (b, vcol)))],
                             n_p, (bp,), HEADS_A, KV_A, DH_A, pl.BlockSpec((t_p, hq), lambda b: (b, 0)))
            tq = 256
            qb0, lat0 = n_p // tq, n_p // t_s
            kc = cache_k_a[:, j].reshape(bs * past, hk)
            vc = cache_v_a[:, j].reshape(bs * past, hk)
            o_s = _attention(q, pl.BlockSpec((tq, hq), lambda b, t: (qb0 + b * (t_s // tq) + t, 0)),
                             [(kc, pl.BlockSpec((past, hk), lambda b, t: (b, 0)),
                               vc, pl.BlockSpec((past, hk), lambda b, t: (b, 0))),
                              (k, pl.BlockSpec((t_s, hk), lambda b, t: (lat0 + b, 0)),
                               p, pl.BlockSpec((t_s, hk), lambda b, t: (lat0 + b, vcol)))],
                             n_s, (bs, t_s // tq), HEADS_A, KV_A, DH_A,
                             pl.BlockSpec((tq, hq), lambda b, t: (b * (t_s // tq) + t, 0)))
            outs["k_a"] = k[:n_p].reshape(bp, 1, t_p, KV_A, DH_A)
            outs["v_a"] = p[:n_p, hq + hk:].reshape(bp, 1, t_p, KV_A, DH_A)
            x = res("plain", (jnp.concatenate([o_p, o_s], axis=0),), w_out_a[j], x, mod, 2, ln_g[i, 0], ln_b[i, 0])
        elif kind == 1:
            p = proj(w_in_b[j])
            raw = proj(w_gate_b[j])
            zeros8 = jnp.zeros((2, HEADS_B), F32)
            a32 = jnp.concatenate([a_log_b[j], zeros8], axis=1).reshape(1, 32)
            dt32 = jnp.concatenate([dt_bias_b[j], zeros8], axis=1).reshape(1, 32)
            gcol, grow = _gate_layouts(_gates("gdn", raw, a32, dt32))
            qp, kp, vp = _prep_b(p, w_conv_b[j], 0, n_p, t_p)
            qs, ks, vs = _prep_b(p, w_conv_b[j], n_p, n_s, t_s)
            o_p, s_p = _gdn(qp, kp, vp, gcol, grow, None, bp, t_p, 0)
            o_s, _ = _gdn(qs, ks, vs, gcol, grow, state_b[:, j], bs, t_s, n_p // CHUNK)
            outs["s_b"] = s_p.reshape(bp, 1, 2, HEADS_B, DK_B, DK_B)
            o2 = jnp.concatenate([o_p, o_s], axis=1)
            x = res("silu", (o2, p, o_norm_b[j].reshape(1, DK_B)), w_out_b[j], x, mod, 2, ln_g[i, 0], ln_b[i, 0],
                    z_colblock=3)
        elif kind == 2:
            p = proj(w_in_c[j])
            hd = HEADS_C * DH_C
            o_p = _attention(p, pl.BlockSpec((t_p, hd), lambda b: (b, 0)),
                             [(p, pl.BlockSpec((t_p, hd), lambda b: (b, 1)),
                               p, pl.BlockSpec((t_p, hd), lambda b: (b, 2)))],
                             n_p, (bp,), HEADS_C, HEADS_C, DH_C, pl.BlockSpec((t_p, hd), lambda b: (b, 0)))
            o_s = _nbr_attention(p, cache_k_c[:, j].reshape(bs * past, hd), cache_v_c[:, j].reshape(bs * past, hd),
                                 rpb_c[j], n_p, bs, t_s, past)
            outs["k_c"] = p[:n_p, hd:2 * hd].reshape(bp, 1, t_p, HEADS_C, DH_C)
            outs["v_c"] = p[:n_p, 2 * hd:].reshape(bp, 1, t_p, HEADS_C, DH_C)
            x = res("plain", (jnp.concatenate([o_p, o_s], axis=0),), w_out_c[j], x, mod, 2, ln_g[i, 0], ln_b[i, 0])
        else:
            p = proj(w_in_d[j])
            raw = proj(w_gate_d[j])
            gcol, grow = _gate_layouts(_gates("mlstm", raw, b_gate_d[j].reshape(1, 32), jnp.zeros((1, 32), F32)))
            h_p, c_p, n_fin, m_fin = _mlstm(p, gcol, grow, None, bp, t_p, 0)
            m0 = jnp.broadcast_to(state_d_m[:, j][..., None], (bs, 2, HEADS_D, 128))
            h_s, _, _, _ = _mlstm(p, gcol, grow, (state_d_c[:, j], state_d_n[:, j], m0), bs, t_s, n_p)
            outs["c_d"] = c_p.reshape(bp, 1, 2, HEADS_D, DK_D, DV_D)
            outs["n_d"] = n_fin.reshape(bp, 1, 2, HEADS_D, DK_D)
            outs["m_d"] = m_fin[..., 0].reshape(bp, 1, 2, HEADS_D)
            h2 = jnp.concatenate([h_p, h_s], axis=1)
            x = res("sigmoid", (h2, p, o_norm_d[j].reshape(1, DV_D)), w_out_d[j], x, mod, 2, ln_g[i, 0], ln_b[i, 0],
                    z_colblock=2)
        x = _moe(x, mod, w_router[i], b_router[i], w_exp_in[i], b_exp_in[i], w_exp_out[i], b_exp_out[i],
                 ln_g[i, 1], ln_b[i, 1], n_p, t_s)
    return (x[:n_p].reshape(bp, t_p, D), x[n_p:].reshape(bs, t_s, D), outs["k_a"], outs["v_a"], outs["s_b"],
            outs["k_c"], outs["v_c"], outs["c_d"], outs["n_d"], outs["m_d"])
```

```python
import functools

import jax
import jax.numpy as jnp
import numpy as np
from jax import lax
from jax.experimental import pallas as pl
from jax.experimental.pallas import tpu as pltpu

F32 = jnp.float32
BF16 = jnp.bfloat16

D = 1024
GRID_W = 64
HEADS_A, KV_A, DH_A = 16, 4, 64
ROPE_THETA = 10000.0
HEADS_B, DK_B = 8, 128
CONV_W = 3
CHUNK = 64
HEADS_C, DH_C = 16, 64
WIN_R, WIN_C = 8, 16
HEADS_D, DK_D, DV_D = 8, 64, 128
N_EXPERTS, TOP_K, D_FF = 32, 4, 1024
SWIGLU_LIMIT, SWIGLU_ALPHA = 7.0, 1.702
NEG_INF = -1e30
N_COND = 8
EXPERT_ROWS = 256
TILE_TOK = 256
SEG_ALIGN = 8
TILE_ROWS = (TILE_TOK * TOP_K + N_EXPERTS * (SEG_ALIGN - 1) + 7) // 8 * 8
VMEM_LIMIT = 56 * 1024 * 1024
_ALPHA = float((2 * 4) ** 0.25)


def _cparams(n_axes):
    return pltpu.CompilerParams(dimension_semantics=("arbitrary",) * n_axes, vmem_limit_bytes=VMEM_LIMIT)


def _dot(a, b):
    return jnp.dot(a.astype(BF16), b.astype(BF16), preferred_element_type=F32)


def _dot_nt(a, b):
    return lax.dot_general(a.astype(BF16), b.astype(BF16), (((1,), (1,)), ((), ())), preferred_element_type=F32)


def _dot_tn(a, b):
    return lax.dot_general(a.astype(BF16), b.astype(BF16), (((0,), (0,)), ((), ())), preferred_element_type=F32)


def _split2(x):
    hi = x.astype(BF16)
    lo = (x - hi.astype(F32)).astype(BF16)
    return hi, lo


def _split3(x):
    hi = x.astype(BF16)
    r = x - hi.astype(F32)
    mid = r.astype(BF16)
    lo = (r - mid.astype(F32)).astype(BF16)
    return hi, mid, lo


def _dot_hi(a, b):
    ah, al = _split2(a)
    bh, bl = _split2(b)
    d = functools.partial(jnp.dot, preferred_element_type=F32)
    return d(ah, bh) + (d(ah, bl) + d(al, bh))


def _mask_dot(mask_bf16, x):
    d = functools.partial(jnp.dot, preferred_element_type=F32)
    hi, mid, lo = _split3(x)
    return d(mask_bf16, hi) + (d(mask_bf16, mid) + d(mask_bf16, lo))


def _dot_mask(x, mask_bf16):
    d = functools.partial(jnp.dot, preferred_element_type=F32)
    hi, mid, lo = _split3(x)
    return d(hi, mask_bf16) + (d(mid, mask_bf16) + d(lo, mask_bf16))


def _layer_norm(x, g, b):
    mu = jnp.mean(x, axis=-1, keepdims=True)
    xc = x - mu
    var = jnp.mean(xc * xc, axis=-1, keepdims=True)
    return xc * lax.rsqrt(var + 1e-5) * g + b


def _sigmoid(x):
    return 1.0 / (1.0 + jnp.exp(-x))


def _cond_index(i, tm, n_prompt, t_lat):
    npt = n_prompt // tm
    return jnp.where(i < npt, 0, 1 + (i - npt) // (t_lat // tm))


def _mod_kernel(c_ref, w_ref, b_ref, o_ref):
    c = c_ref[...]
    o_ref[...] = _dot(c * _sigmoid(c), w_ref[...]) + b_ref[...]


def _modulation_table(cond, w, b):
    n = w.shape[1]
    tn = 1024
    return pl.pallas_call(
        _mod_kernel,
        out_shape=jax.ShapeDtypeStruct((N_COND, n), F32),
        grid=(n // tn,),
        in_specs=[pl.BlockSpec((N_COND, D), lambda j: (0, 0)),
                  pl.BlockSpec((D, tn), lambda j: (0, j)),
                  pl.BlockSpec((1, tn), lambda j: (0, j))],
        out_specs=pl.BlockSpec((N_COND, tn), lambda j: (0, j)),
        compiler_params=_cparams(1),
        name="mod_table",
    )(cond, w, b.reshape(1, n))


def _modmm_kernel(x_ref, sc_ref, sh_ref, w_ref, o_ref, wbf_ref):
    @pl.when(pl.program_id(1) == 0)
    def _():
        wbf_ref[...] = w_ref[...].astype(BF16)

    h = x_ref[...] * (1.0 + sc_ref[0]) + sh_ref[0]
    o_ref[...] = jnp.dot(h.astype(BF16), wbf_ref[...], preferred_element_type=F32)


def _col_tile(n):
    for t in (1024, 512, 256, 128):
        if n % t == 0:
            return t
    return n


def _modmm(x, mod, which_sc, which_sh, w, n_prompt, t_lat, tm=512):
    m = x.shape[0]
    n = w.shape[1]
    tn = _col_tile(n)
    cidx = functools.partial(_cond_index, tm=tm, n_prompt=n_prompt, t_lat=t_lat)
    return pl.pallas_call(
        _modmm_kernel,
        out_shape=jax.ShapeDtypeStruct((m, n), F32),
        grid=(n // tn, m // tm),
        in_specs=[pl.BlockSpec((tm, D), lambda j, i: (i, 0)),
                  pl.BlockSpec((1, 1, D), lambda j, i: (cidx(i), 0, which_sc)),
                  pl.BlockSpec((1, 1, D), lambda j, i: (cidx(i), 0, which_sh)),
                  pl.BlockSpec((D, tn), lambda j, i: (0, j))],
        out_specs=pl.BlockSpec((tm, tn), lambda j, i: (i, j)),
        scratch_shapes=[pltpu.VMEM((D, tn), BF16)],
        compiler_params=_cparams(2),
        name="mod_proj",
    )(x, mod, mod, w)


def _proj_kernel(*refs, kind):
    if kind == "plain":
        a_ref, w_ref, x_ref, g_ref, lg_ref, lb_ref, o_ref, wbf_ref = refs
    else:
        s_ref, z_ref, on_ref, w_ref, x_ref, g_ref, lg_ref, lb_ref, o_ref, wbf_ref, abuf_ref = refs

    @pl.when(pl.program_id(0) == 0)
    def _():
        wbf_ref[...] = w_ref[...].astype(BF16)

    if kind == "plain":
        a = a_ref[...].astype(BF16)
    else:
        s = s_ref[0] + s_ref[1]
        on = on_ref[...]
        hd = on.shape[-1]
        for h in range(D // hd):
            sl = slice(h * hd, (h + 1) * hd)
            sh = s[:, sl]
            y = sh * lax.rsqrt(jnp.mean(sh * sh, axis=-1, keepdims=True) + 1e-6) * on
            z = z_ref[:, sl]
            gate = _sigmoid(z)
            if kind == "silu":
                gate = z * gate
            abuf_ref[:, sl] = (y * gate).astype(BF16)
        a = abuf_ref[...]
    y = jnp.dot(a, wbf_ref[...], preferred_element_type=F32)
    o_ref[...] = _layer_norm(_ALPHA * x_ref[...] + g_ref[0] * y, lg_ref[...], lb_ref[...])


def _proj_res_ln(kind, a_args, w, x, mod, which_g, ln_g, ln_b, n_prompt, t_lat, z_colblock=0, tm=256):
    m = x.shape[0]
    cidx = functools.partial(_cond_index, tm=tm, n_prompt=n_prompt, t_lat=t_lat)
    row = pl.BlockSpec((tm, D), lambda i: (i, 0))
    vec = pl.BlockSpec((1, D), lambda i: (0, 0))
    if kind == "plain":
        a_specs = [row]
        scratch = [pltpu.VMEM((D, D), BF16)]
    else:
        on = a_args[2]
        a_specs = [pl.BlockSpec((2, tm, D), lambda i: (0, i, 0)),
                   pl.BlockSpec((tm, D), lambda i: (i, z_colblock)),
                   pl.BlockSpec((1, on.shape[-1]), lambda i: (0, 0))]
        scratch = [pltpu.VMEM((D, D), BF16), pltpu.VMEM((tm, D), BF16)]
    return pl.pallas_call(
        functools.partial(_proj_kernel, kind=kind),
        out_shape=jax.ShapeDtypeStruct((m, D), F32),
        grid=(m // tm,),
        in_specs=a_specs + [pl.BlockSpec((D, D), lambda i: (0, 0)), row,
                            pl.BlockSpec((1, 1, D), lambda i: (cidx(i), 0, which_g)), vec, vec],
        out_specs=row,
        scratch_shapes=scratch,
        compiler_params=_cparams(1),
        name="out_proj_" + kind,
    )(*a_args, w, x, mod, ln_g.reshape(1, D), ln_b.reshape(1, D))


def _prep_a_kernel(p_ref, qn_ref, kn_ref, cos_ref, sin_ref, q_ref, k_ref):
    cos = cos_ref[...]
    sin = sin_ref[...]
    half = DH_A // 2

    def one(x, g):
        y = x * lax.rsqrt(jnp.mean(x * x, axis=-1, keepdims=True) + 1e-6) * g
        swapped = jnp.concatenate([y[:, half:], y[:, :half]], axis=-1)
        return y * cos + swapped * sin

    for h in range(HEADS_A):
        sl = slice(h * DH_A, (h + 1) * DH_A)
        q_ref[:, sl] = one(p_ref[:, sl], qn_ref[...])
    for h in range(KV_A):
        sl = slice(h * DH_A, (h + 1) * DH_A)
        k_ref[:, sl] = one(p_ref[:, HEADS_A * DH_A + h * DH_A: HEADS_A * DH_A + (h + 1) * DH_A], kn_ref[...])


def _rope_tables(t_lat, tm):
    quarter = DH_A // 4
    t = np.arange(t_lat)
    inv_freq = ROPE_THETA ** (-np.arange(quarter, dtype=np.float32) / quarter)
    ang = np.concatenate([(t // GRID_W)[:, None] * inv_freq, (t % GRID_W)[:, None] * inv_freq], axis=-1)
    ang = jnp.asarray(ang, F32)
    cos, sin = jnp.cos(ang), jnp.sin(ang)
    cos64 = jnp.concatenate([cos, cos], axis=-1)
    sin64 = jnp.concatenate([-sin, sin], axis=-1)
    cos64 = jnp.concatenate([jnp.ones((tm, DH_A), F32), cos64], axis=0)
    sin64 = jnp.concatenate([jnp.zeros((tm, DH_A), F32), sin64], axis=0)
    return cos64, sin64


def _prep_a(p, q_norm, k_norm, n_prompt, t_lat, tm=256):
    m = p.shape[0]
    cos64, sin64 = _rope_tables(t_lat, tm)
    npt = n_prompt // tm

    def tab(i):
        return (jnp.where(i < npt, 0, 1 + (i - npt) % (t_lat // tm)), 0)

    return pl.pallas_call(
        _prep_a_kernel,
        out_shape=(jax.ShapeDtypeStruct((m, HEADS_A * DH_A), F32), jax.ShapeDtypeStruct((m, KV_A * DH_A), F32)),
        grid=(m // tm,),
        in_specs=[pl.BlockSpec((tm, p.shape[1]), lambda i: (i, 0)),
                  pl.BlockSpec((1, DH_A), lambda i: (0, 0)),
                  pl.BlockSpec((1, DH_A), lambda i: (0, 0)),
                  pl.BlockSpec((tm, DH_A), tab),
                  pl.BlockSpec((tm, DH_A), tab)],
        out_specs=(pl.BlockSpec((tm, HEADS_A * DH_A), lambda i: (i, 0)),
                   pl.BlockSpec((tm, KV_A * DH_A), lambda i: (i, 0))),
        compiler_params=_cparams(1),
        name="prep_a",
    )(p, q_norm.reshape(1, DH_A), k_norm.reshape(1, DH_A), cos64, sin64)


def _softmax_pv(scores, values):
    m = scores[0].max(axis=-1, keepdims=True)
    for s in scores[1:]:
        m = jnp.maximum(m, s.max(axis=-1, keepdims=True))
    ps = [jnp.exp(s - m) for s in scores]
    den = ps[0].sum(axis=-1, keepdims=True)
    for p in ps[1:]:
        den = den + p.sum(axis=-1, keepdims=True)
    inv = 1.0 / den
    o = _dot(ps[0] * inv, values[0])
    for p, v in zip(ps[1:], values[1:]):
        o = o + _dot(p * inv, v)
    return o


def _pipelined_heads(n_heads, score_fn, finish_fn, batch=4):
    pending = []
    for h0 in range(0, n_heads, batch):
        current = [(h, score_fn(h)) for h in range(h0, min(h0 + batch, n_heads))]
        for h, sc in pending:
            finish_fn(h, sc)
        pending = current
    for h, sc in pending:
        finish_fn(h, sc)


def _attn_kernel(*refs, heads, kv_heads, dh, nseg):
    q_ref = refs[0]
    kv = refs[1:1 + 2 * nseg]
    o_ref = refs[1 + 2 * nseg]
    scale = dh ** -0.5
    group = heads // kv_heads

    def scores(h):
        ksl = slice(h // group * dh, (h // group + 1) * dh)
        q = q_ref[:, h * dh:(h + 1) * dh]
        return [_dot_nt(q, kv[2 * s][:, ksl]) * scale for s in range(nseg)]

    def finish(h, sc):
        ksl = slice(h // group * dh, (h // group + 1) * dh)
        o_ref[:, h * dh:(h + 1) * dh] = _softmax_pv(sc, [kv[2 * s + 1][:, ksl] for s in range(nseg)])

    _pipelined_heads(heads, scores, finish)


def _attention(q_arr, q_spec, segs, out_rows, grid, heads, kv_heads, dh, out_spec):
    in_specs = [q_spec]
    args = [q_arr]
    for (k_arr, k_spec, v_arr, v_spec) in segs:
        in_specs += [k_spec, v_spec]
        args += [k_arr, v_arr]
    return pl.pallas_call(
        functools.partial(_attn_kernel, heads=heads, kv_heads=kv_heads, dh=dh, nseg=len(segs)),
        out_shape=jax.ShapeDtypeStruct((out_rows, heads * dh), F32),
        grid=grid,
        in_specs=in_specs,
        out_specs=out_spec,
        compiler_params=_cparams(len(grid)),
        name="attention",
    )(*args)


def _window_start(r, n_rows):
    wr = min(WIN_R, n_rows)
    return jnp.minimum(jnp.maximum(r - wr // 2, 0), n_rows - wr)


def _nbr_kernel(q_ref, kl_ref, vl_ref, kc_ref, vc_ref, bias_ref, o_ref, *, n_rows):
    r = pl.program_id(1)
    wr = min(WIN_R, n_rows)
    start = pl.multiple_of(_window_start(r, n_rows) * GRID_W, GRID_W)
    scale = DH_C ** -0.5
    window = pl.ds(start, wr * GRID_W)

    def scores(h):
        sl = slice(h * DH_C, (h + 1) * DH_C)
        q = q_ref[:, sl]
        return [_dot_nt(q, kl_ref[window, sl]) * scale + bias_ref[0, h], _dot_nt(q, kc_ref[:, sl]) * scale]

    def finish(h, sc):
        sl = slice(h * DH_C, (h + 1) * DH_C)
        o_ref[:, sl] = _softmax_pv(sc, [vl_ref[window, sl], vc_ref[:, sl]])

    _pipelined_heads(HEADS_C, scores, finish)


def _nbr_bias(rpb, n_rows):
    wr = min(WIN_R, n_rows)
    cols = np.arange(GRID_W)
    col_start = np.clip(cols - WIN_C // 2, 0, GRID_W - WIN_C)
    col_ok = (cols[None, :] >= col_start[:, None]) & (cols[None, :] < col_start[:, None] + WIN_C)
    dc = np.clip(cols[None, :] - cols[:, None], -(WIN_C - 1), WIN_C - 1) + (WIN_C - 1)
    n_off = 2 * WIN_R - wr
    pick = np.zeros((2 * WIN_C - 1, GRID_W * GRID_W), np.float32)
    pick[dc.reshape(-1), np.arange(GRID_W * GRID_W)] = 1.0
    tab = jnp.einsum("hrd,dx->hrx", rpb.astype(F32), jnp.asarray(pick), precision=lax.Precision.HIGHEST)
    tab = jnp.where(col_ok[None, None], tab.reshape(HEADS_C, 2 * WIN_R - 1, GRID_W, GRID_W), NEG_INF)
    b = jnp.stack([tab[:, off:off + wr] for off in range(n_off)], axis=0)
    return jnp.transpose(b, (0, 1, 3, 2, 4)).reshape(n_off, HEADS_C, GRID_W, wr * GRID_W)


def _nbr_attention(q, k, v, k_ctx, v_ctx, rpb, n_prompt, n_lat_b, t_lat, past):
    n_rows = t_lat // GRID_W
    wr = min(WIN_R, n_rows)
    bias = _nbr_bias(rpb, n_rows)
    row0 = n_prompt // GRID_W
    lat0 = n_prompt // t_lat
    hd = HEADS_C * DH_C

    def off(b, r):
        return (_window_start(r, n_rows) - r + (WIN_R - 1), 0, 0, 0)

    return pl.pallas_call(
        functools.partial(_nbr_kernel, n_rows=n_rows),
        out_shape=jax.ShapeDtypeStruct((n_lat_b * t_lat, hd), F32),
        grid=(n_lat_b, n_rows),
        in_specs=[pl.BlockSpec((GRID_W, hd), lambda b, r: (row0 + b * n_rows + r, 0)),
                  pl.BlockSpec((t_lat, hd), lambda b, r: (lat0 + b, 0)),
                  pl.BlockSpec((t_lat, hd), lambda b, r: (lat0 + b, 0)),
                  pl.BlockSpec((past, hd), lambda b, r: (b, 0)),
                  pl.BlockSpec((past, hd), lambda b, r: (b, 0)),
                  pl.BlockSpec((1, HEADS_C, GRID_W, wr * GRID_W), off)],
        out_specs=pl.BlockSpec((GRID_W, hd), lambda b, r: (b * n_rows + r, 0)),
        compiler_params=_cparams(2),
        name="nbr_attention",
    )(q, k, v, k_ctx, v_ctx, bias)


def _chunk_masks(d):
    sgn = 1 - 2 * d
    r = lax.broadcasted_iota(jnp.int32, (CHUNK, CHUNK), 0)
    c = lax.broadcasted_iota(jnp.int32, (CHUNK, CHUNK), 1)
    diff = (r - c) * sgn
    incl = diff >= 0
    strict = diff > 0
    incl_t = diff <= 0
    return incl, strict, incl_t


def _gate_kernel(raw_ref, p0_ref, p1_ref, o_ref, *, kind):
    raw = raw_ref[...]
    lane = lax.broadcasted_iota(jnp.int32, raw.shape, 1)
    first = (lane % 16) < 8
    if kind == "gdn":
        g = -jnp.exp(p0_ref[...]) * jax.nn.softplus(raw + p1_ref[...])
        o_ref[...] = jnp.where(first, g, _sigmoid(raw))
    else:
        x = raw + p0_ref[...]
        o_ref[...] = jnp.where(first, x, -jax.nn.softplus(-x))


def _gates(kind, raw, p0, p1, tm=512):
    m, n = raw.shape
    return pl.pallas_call(
        functools.partial(_gate_kernel, kind=kind),
        out_shape=jax.ShapeDtypeStruct((m, n), F32),
        grid=(m // tm,),
        in_specs=[pl.BlockSpec((tm, n), lambda i: (i, 0)),
                  pl.BlockSpec((1, n), lambda i: (0, 0)),
                  pl.BlockSpec((1, n), lambda i: (0, 0))],
        out_specs=pl.BlockSpec((tm, n), lambda i: (i, 0)),
        compiler_params=_cparams(1),
        name="gates_" + kind,
    )(raw, p0, p1)


def _gate_layouts(g):
    m = g.shape[0]
    col = g.reshape(m // CHUNK, CHUNK, 2, 16).transpose(2, 0, 1, 3)
    return col, col.transpose(0, 1, 3, 2)


def _prep_b_kernel(p_ref, prev_ref, next_ref, w_ref, q_ref, k_ref, v_ref, *, tm, tiles_per_seq):
    i = pl.program_id(0)
    pos = i % tiles_per_seq
    rows = lax.broadcasted_iota(jnp.int32, (tm, D), 0)
    outs = (q_ref, k_ref, v_ref)
    for seg in range(3):
        sl = slice(seg * D, (seg + 1) * D)
        x = p_ref[:, sl]
        prev_row = jnp.where(pos == 0, 0.0, prev_ref[7:8, sl])
        next_row = jnp.where(pos == tiles_per_seq - 1, 0.0, next_ref[0:1, sl])
        x_prev = jnp.where(rows == 0, prev_row, pltpu.roll(x, 1, axis=0))
        x_next = jnp.where(rows == tm - 1, next_row, pltpu.roll(x, tm - 1, axis=0))
        y = x_prev * w_ref[0:1, sl] + x * w_ref[1:2, sl] + x_next * w_ref[2:3, sl]
        y = y * _sigmoid(y)
        if seg < 2:
            for h in range(HEADS_B):
                hs = slice(h * DK_B, (h + 1) * DK_B)
                yh = y[:, hs]
                outs[seg][:, hs] = yh * lax.rsqrt(jnp.sum(yh * yh, axis=-1, keepdims=True) + 1e-6)
        else:
            v_ref[...] = y


def _prep_b(p, w_conv, row0, n_rows, t_seq, tm=256):
    total8 = p.shape[0] // 8
    r0 = row0 // tm
    tiles_per_seq = t_seq // tm
    out = jax.ShapeDtypeStruct((n_rows, D), F32)
    ospec = pl.BlockSpec((tm, D), lambda i: (i, 0))
    return pl.pallas_call(
        functools.partial(_prep_b_kernel, tm=tm, tiles_per_seq=tiles_per_seq),
        out_shape=(out, out, out),
        grid=(n_rows // tm,),
        in_specs=[pl.BlockSpec((tm, p.shape[1]), lambda i: (r0 + i, 0)),
                  pl.BlockSpec((8, p.shape[1]), lambda i: (jnp.maximum((r0 + i) * (tm // 8) - 1, 0), 0)),
                  pl.BlockSpec((8, p.shape[1]), lambda i: (jnp.minimum((r0 + i + 1) * (tm // 8), total8 - 1), 0)),
                  pl.BlockSpec((CONV_W, 3 * D), lambda i: (0, 0))],
        out_specs=(ospec, ospec, ospec),
        compiler_params=_cparams(1),
        name="prep_b",
    )(p, p, p, w_conv)


def _unit_tri_inverses(mats):
    r = lax.broadcasted_iota(jnp.int32, mats[0].shape, 0)
    c = lax.broadcasted_iota(jnp.int32, mats[0].shape, 1)
    eye = jnp.where(r == c, 1.0, 0.0)
    ps = [-a for a in mats]
    ts = [eye + p for p in ps]
    for _ in range(int(np.log2(CHUNK)) - 1):
        ps = [_dot_hi(p, p) for p in ps]
        ts = [t + _dot_hi(t, p) for t, p in zip(ts, ps)]
    return ts


def _gdn_kernel(*refs, n_chunks, zero_init):
    if zero_init:
        q_ref, k_ref, v_ref, gc_ref, gr_ref, o_ref, sfin_ref, s_ref = refs
    else:
        q_ref, k_ref, v_ref, gc_ref, gr_ref, s0_ref, o_ref, sfin_ref, s_ref = refs
    d = pl.program_id(1)
    c = pl.program_id(2)

    @pl.when(c == 0)
    def _():
        if zero_init:
            s_ref[...] = jnp.zeros(s_ref.shape, F32)
        else:
            s_ref[...] = s0_ref[0, 0]

    incl, strict, incl_t = _chunk_masks(d)
    m_incl = jnp.where(incl, 1.0, 0.0).astype(BF16)
    m_incl_t = jnp.where(incl_t, 1.0, 0.0).astype(BF16)
    gcol = gc_ref[0, 0]
    grow = gr_ref[0, 0]
    gcs = _mask_dot(m_incl, gcol)
    grs = _dot_mask(grow, m_incl_t)
    last = jnp.where(d == 0, CHUNK - 1, 0)
    rowi = lax.broadcasted_iota(jnp.int32, gcs.shape, 0)
    g_last = jnp.sum(jnp.where(rowi == last, gcs, 0.0), axis=0, keepdims=True)
    scale = DK_B ** -0.5
    heads = range(HEADS_B)
    q_all, k_all, v_all = q_ref[...], k_ref[...], v_ref[...]
    hs = [slice(h * DK_B, (h + 1) * DK_B) for h in heads]
    qh = [q_all[:, s] * scale for s in hs]
    kh = [k_all[:, s] for s in hs]
    vh = [v_all[:, s] for s in hs]
    gc_c = [gcs[:, h:h + 1] for h in heads]
    b_c = [gcol[:, HEADS_B + h:HEADS_B + h + 1] for h in heads]
    gl = [g_last[:, h:h + 1] for h in heads]
    states = [s_ref[h] for h in heads]
    decay = [jnp.exp(jnp.where(incl, gc_c[h] - grs[h:h + 1, :], NEG_INF)) for h in heads]
    egc = [jnp.exp(g) for g in gc_c]
    kb = [kh[h] * b_c[h] for h in heads]
    a_mat = [jnp.where(strict, _dot_nt(kb[h], kh[h]) * decay[h], 0.0) for h in heads]
    attn = [_dot_nt(qh[h], kh[h]) * decay[h] for h in heads]
    o_state = [_dot(qh[h] * egc[h], states[h]) for h in heads]
    t_inv = _unit_tri_inverses(a_mat)
    u = [_dot_hi(t_inv[h], vh[h] * b_c[h]) for h in heads]
    w = [_dot_hi(t_inv[h], kb[h] * egc[h]) for h in heads]
    v_new = [u[h] - _dot(w[h], states[h]) for h in heads]
    o_ref[0] = jnp.concatenate([o_state[h] + _dot(attn[h], v_new[h]) for h in heads], axis=1)
    for h in heads:
        s_ref[h] = states[h] * jnp.exp(gl[h]) + _dot_tn(kh[h] * jnp.exp(gl[h] - gc_c[h]), v_new[h])

    @pl.when(c == n_chunks - 1)
    def _():
        sfin_ref[0, 0] = s_ref[...]


def _chunk_index(d, c, n_chunks):
    return c + d * (n_chunks - 1 - 2 * c)


def _gdn(q, k, v, gcol, grow, s0, n_seq, t_seq, chunk0):
    n_chunks = t_seq // CHUNK

    def rows(b, d, c):
        return (b * n_chunks + _chunk_index(d, c, n_chunks), 0)

    def gidx(b, d, c):
        return (d, chunk0 + b * n_chunks + _chunk_index(d, c, n_chunks), 0, 0)

    blk = pl.BlockSpec((CHUNK, D), rows)
    in_specs = [blk, blk, blk,
                pl.BlockSpec((1, 1, CHUNK, 16), gidx),
                pl.BlockSpec((1, 1, 16, CHUNK), gidx)]
    args = [q, k, v, gcol, grow]
    state_spec = pl.BlockSpec((1, 1, HEADS_B, DK_B, DK_B), lambda b, d, c: (b, d, 0, 0, 0))
    if s0 is not None:
        in_specs.append(state_spec)
        args.append(s0)
    m = n_seq * t_seq
    return pl.pallas_call(
        functools.partial(_gdn_kernel, n_chunks=n_chunks, zero_init=s0 is None),
        out_shape=(jax.ShapeDtypeStruct((2, m, D), F32),
                   jax.ShapeDtypeStruct((n_seq, 2, HEADS_B, DK_B, DK_B), F32)),
        grid=(n_seq, 2, n_chunks),
        in_specs=in_specs,
        out_specs=(pl.BlockSpec((1, CHUNK, D), lambda b, d, c: (d,) + rows(b, d, c)), state_spec),
        scratch_shapes=[pltpu.VMEM((HEADS_B, DK_B, DK_B), F32)],
        compiler_params=_cparams(3),
        name="gdn",
    )(*args)


def _mlstm_kernel(*refs, n_chunks, zero_init):
    if zero_init:
        qk_ref, v_ref, gc_ref, gr_ref, o_ref, cfin_ref, nfin_ref, mfin_ref, c_ref, n_ref, m_ref = refs
    else:
        (qk_ref, v_ref, gc_ref, gr_ref, c0_ref, n0_ref, m0_ref,
         o_ref, cfin_ref, nfin_ref, mfin_ref, c_ref, n_ref, m_ref) = refs
    d = pl.program_id(1)
    c = pl.program_id(2)

    @pl.when(c == 0)
    def _():
        if zero_init:
            c_ref[...] = jnp.zeros(c_ref.shape, F32)
            n_ref[...] = jnp.zeros(n_ref.shape, F32)
            m_ref[...] = jnp.zeros(m_ref.shape, F32)
        else:
            c_ref[...] = c0_ref[0, 0]
            n_ref[...] = n0_ref[0, 0]
            m_ref[...] = m0_ref[0, 0]

    incl, _, incl_t = _chunk_masks(d)
    m_incl = jnp.where(incl, 1.0, 0.0).astype(BF16)
    m_incl_t = jnp.where(incl_t, 1.0, 0.0).astype(BF16)
    gcol = gc_ref[0, 0]
    grow = gr_ref[0, 0]
    bcs = _mask_dot(m_incl, gcol)
    brs = _dot_mask(grow, m_incl_t)
    last = jnp.where(d == 0, CHUNK - 1, 0)
    rowi = lax.broadcasted_iota(jnp.int32, bcs.shape, 0)
    b_last_all = jnp.sum(jnp.where(rowi == last, bcs, 0.0), axis=0, keepdims=True)
    scale = DK_D ** -0.5
    heads = range(HEADS_D)
    qk_all, v_all = qk_ref[...], v_ref[...]
    m_all, n_all = m_ref[...], n_ref[...]
    c_old = [c_ref[h] for h in heads]
    qh = [qk_all[:, h * DK_D:(h + 1) * DK_D] * scale for h in heads]
    kh = [qk_all[:, HEADS_D * DK_D + h * DK_D: HEADS_D * DK_D + (h + 1) * DK_D] for h in heads]
    vh = [v_all[:, h * DV_D:(h + 1) * DV_D] for h in heads]
    b_c = [bcs[:, HEADS_D + h:HEADS_D + h + 1] for h in heads]
    ig_c = [gcol[:, h:h + 1] for h in heads]
    m_old = [m_all[h:h + 1, 0:1] for h in heads]
    n_old = [n_all[h:h + 1, :] for h in heads]
    a_inter = [b_c[h] + m_old[h] for h in heads]
    d_intra = [jnp.where(incl, b_c[h] - brs[HEADS_D + h:HEADS_D + h + 1, :] + grow[h:h + 1, :], NEG_INF)
               for h in heads]
    m_t = [jnp.maximum(a_inter[h], jnp.max(d_intra[h], axis=-1, keepdims=True)) for h in heads]
    w_inter = [jnp.exp(a_inter[h] - m_t[h]) for h in heads]
    s = [_dot_nt(qh[h], kh[h]) * jnp.exp(d_intra[h] - m_t[h]) for h in heads]
    q_state = [_dot(qh[h], c_old[h]) for h in heads]
    b_last = [b_last_all[:, HEADS_D + h:HEADS_D + h + 1] for h in heads]
    d_state = [b_last[h] - b_c[h] + ig_c[h] for h in heads]
    m_new = [jnp.maximum(b_last[h] + m_old[h], jnp.max(d_state[h], axis=0, keepdims=True)) for h in heads]
    w_old = [jnp.exp(b_last[h] + m_old[h] - m_new[h]) for h in heads]
    kw = [kh[h] * jnp.exp(d_state[h] - m_new[h]) for h in heads]
    kv = [_dot_tn(kw[h], vh[h]) for h in heads]
    outs = []
    for h in heads:
        num = w_inter[h] * q_state[h] + _dot(s[h], vh[h])
        qn = jnp.sum(qh[h] * n_old[h], axis=-1, keepdims=True)
        den = w_inter[h] * qn + jnp.sum(s[h], axis=-1, keepdims=True)
        outs.append(num / jnp.maximum(jnp.abs(den), jnp.exp(-m_t[h])))
    o_ref[0] = jnp.concatenate(outs, axis=1)
    for h in heads:
        c_ref[h] = w_old[h] * c_old[h] + kv[h]
    n_ref[...] = jnp.concatenate([w_old[h] * n_old[h] + jnp.sum(kw[h], axis=0, keepdims=True) for h in heads], axis=0)
    m_ref[...] = jnp.concatenate([jnp.broadcast_to(m_new[h], (1, m_ref.shape[1])) for h in heads], axis=0)

    @pl.when(c == n_chunks - 1)
    def _():
        cfin_ref[0, 0] = c_ref[...]
        nfin_ref[0, 0] = n_ref[...]
        mfin_ref[0, 0] = m_ref[...]


def _mlstm(p, gcol, grow, states, n_seq, t_seq, row0):
    n_chunks = t_seq // CHUNK
    chunk0 = row0 // CHUNK

    def rows(b, d, c):
        return chunk0 + b * n_chunks + _chunk_index(d, c, n_chunks)

    def gidx(b, d, c):
        return (d, rows(b, d, c), 0, 0)

    in_specs = [pl.BlockSpec((CHUNK, D), lambda b, d, c: (rows(b, d, c), 0)),
                pl.BlockSpec((CHUNK, D), lambda b, d, c: (rows(b, d, c), 1)),
                pl.BlockSpec((1, 1, CHUNK, 16), gidx),
                pl.BlockSpec((1, 1, 16, CHUNK), gidx)]
    args = [p, p, gcol, grow]
    c_spec = pl.BlockSpec((1, 1, HEADS_D, DK_D, DV_D), lambda b, d, c: (b, d, 0, 0, 0))
    n_spec = pl.BlockSpec((1, 1, HEADS_D, DK_D), lambda b, d, c: (b, d, 0, 0))
    m_spec = pl.BlockSpec((1, 1, HEADS_D, 128), lambda b, d, c: (b, d, 0, 0))
    if states is not None:
        in_specs += [c_spec, n_spec, m_spec]
        args += list(states)
    m = n_seq * t_seq
    return pl.pallas_call(
        functools.partial(_mlstm_kernel, n_chunks=n_chunks, zero_init=states is None),
        out_shape=(jax.ShapeDtypeStruct((2, m, D), F32),
                   jax.ShapeDtypeStruct((n_seq, 2, HEADS_D, DK_D, DV_D), F32),
                   jax.ShapeDtypeStruct((n_seq, 2, HEADS_D, DK_D), F32),
                   jax.ShapeDtypeStruct((n_seq, 2, HEADS_D, 128), F32)),
        grid=(n_seq, 2, n_chunks),
        in_specs=in_specs,
        out_specs=(pl.BlockSpec((1, CHUNK, D),
                                lambda b, d, c: (d, b * n_chunks + _chunk_index(d, c, n_chunks), 0)),
                   c_spec, n_spec, m_spec),
        scratch_shapes=[pltpu.VMEM((HEADS_D, DK_D, DV_D), F32),
                        pltpu.VMEM((HEADS_D, DK_D), F32),
                        pltpu.VMEM((HEADS_D, 128), F32)],
        compiler_params=_cparams(3),
        name="mlstm",
    )(*args)


def _router_kernel(x_ref, sc_ref, sh_ref, wt_ref, b_ref, xs_ref, pos_ref, gate_ref, cnt_ref):
    h = (x_ref[...] * (1.0 + sc_ref[0]) + sh_ref[0]).astype(BF16)
    logits = _dot_nt(wt_ref[...], h) + b_ref[...]
    sub = lax.broadcasted_iota(jnp.int32, logits.shape, 0)
    onehots, vals = [], []
    for k in range(TOP_K):
        m = jnp.max(logits, axis=0, keepdims=True)
        idx = jnp.min(jnp.where(logits == m, sub, N_EXPERTS), axis=0, keepdims=True)
        hit = sub == idx
        logits = jnp.where(hit, -jnp.inf, logits)
        onehots.append(jnp.where(hit, 1.0, 0.0))
        vals.append(m)
    exps = [jnp.exp(v - vals[0]) for v in vals]
    den = exps[0] + exps[1] + exps[2] + exps[3]
    counts = [jnp.sum(o, axis=1, keepdims=True) for o in onehots]
    n = counts[0] + counts[1] + counts[2] + counts[3]
    n_al = jnp.floor((n + (SEG_ALIGN - 1)) * (1.0 / SEG_ALIGN)) * SEG_ALIGN
    er = lax.broadcasted_iota(jnp.int32, (N_EXPERTS, N_EXPERTS), 0)
    ec = lax.broadcasted_iota(jnp.int32, (N_EXPERTS, N_EXPERTS), 1)
    before_e = jnp.where(ec < er, 1.0, 0.0).astype(BF16)
    start = jnp.dot(before_e, jnp.broadcast_to(n_al, logits.shape).astype(BF16), preferred_element_type=F32)
    tr = lax.broadcasted_iota(jnp.int32, (TILE_TOK, TILE_TOK), 0)
    tc = lax.broadcasted_iota(jnp.int32, (TILE_TOK, TILE_TOK), 1)
    before_t = jnp.where(tr < tc, 1.0, 0.0).astype(BF16)
    rows = lax.broadcasted_iota(jnp.int32, (TILE_ROWS, TILE_TOK), 0)
    perm = jnp.zeros((TILE_ROWS, TILE_TOK), F32)
    prev = jnp.zeros_like(n)
    for k in range(TOP_K):
        cum = jnp.dot(onehots[k].astype(BF16), before_t, preferred_element_type=F32)
        pos = jnp.sum(onehots[k] * (start + prev + cum), axis=0, keepdims=True).astype(jnp.int32)
        prev = prev + counts[k]
        pos_ref[0, k:k + 1, :] = pos
        gate_ref[0, k:k + 1, :] = exps[k] / den
        perm = perm + jnp.where(rows == pos, 1.0, 0.0)
    xs_ref[...] = jnp.dot(perm.astype(BF16), h, preferred_element_type=F32)
    cnt_ref[0] = jnp.broadcast_to(n_al, cnt_ref.shape[1:]).astype(jnp.int32)


def _router(x, mod, w, b, n_prompt, t_lat):
    m = x.shape[0]
    nt = m // TILE_TOK
    cidx = functools.partial(_cond_index, tm=TILE_TOK, n_prompt=n_prompt, t_lat=t_lat)
    sel = pl.BlockSpec((1, TOP_K, TILE_TOK), lambda i: (i, 0, 0))
    return pl.pallas_call(
        _router_kernel,
        out_shape=(jax.ShapeDtypeStruct((nt * TILE_ROWS, D), F32),
                   jax.ShapeDtypeStruct((nt, TOP_K, TILE_TOK), jnp.int32),
                   jax.ShapeDtypeStruct((nt, TOP_K, TILE_TOK), F32),
                   jax.ShapeDtypeStruct((nt, N_EXPERTS, 128), jnp.int32)),
        grid=(nt,),
        in_specs=[pl.BlockSpec((TILE_TOK, D), lambda i: (i, 0)),
                  pl.BlockSpec((1, 1, D), lambda i: (cidx(i), 0, 4)),
                  pl.BlockSpec((1, 1, D), lambda i: (cidx(i), 0, 3)),
                  pl.BlockSpec((N_EXPERTS, D), lambda i: (0, 0)),
                  pl.BlockSpec((N_EXPERTS, 1), lambda i: (0, 0))],
        out_specs=(pl.BlockSpec((TILE_ROWS, D), lambda i: (i, 0)), sel, sel,
                   pl.BlockSpec((1, N_EXPERTS, 128), lambda i: (i, 0, 0))),
        compiler_params=_cparams(1),
        name="router",
    )(x, mod, mod, w.T, b.reshape(N_EXPERTS, 1))


def _expert_kernel(be_ref, boff_ref, brows_ref, tlo_ref, thi_ref, nb_ref, scum_ref, sn_ref, ssrc_ref, tused_ref,
                   xs_hbm, win_ref, bin_ref, wout_ref, bout_ref, ys_hbm,
                   xbuf, ybuf, sem_in, sem_out, winbf, woutbf, *, n_tiles):
    i = pl.program_id(0)
    nb = nb_ref[0]
    slot = i % 2

    def for_pieces(b, fn):
        e = be_ref[b]
        off = boff_ref[b]

        def body(t, carry):
            j = e * n_tiles + t
            s0 = scum_ref[j]
            lo = jnp.maximum(s0, off)
            hi = jnp.minimum(s0 + sn_ref[j], off + EXPERT_ROWS)

            @pl.when(hi > lo)
            def _():
                fn(pl.multiple_of(ssrc_ref[j] + (lo - s0), SEG_ALIGN), pl.multiple_of(lo - off, SEG_ALIGN),
                   pl.multiple_of(hi - lo, SEG_ALIGN))
            return carry
        lax.fori_loop(tlo_ref[b], thi_ref[b], body, 0)

    def in_copy(s, src, dst, n):
        return pltpu.make_async_copy(xs_hbm.at[pl.ds(src, n), :], xbuf.at[s, pl.ds(dst, n), :], sem_in.at[s])

    def out_copy(s, src, dst, n):
        return pltpu.make_async_copy(ybuf.at[s, pl.ds(dst, n), :], ys_hbm.at[pl.ds(src, n), :], sem_out.at[s])

    def wait_block(copy, b, s):
        n = pl.multiple_of(brows_ref[b], SEG_ALIGN)

        @pl.when(n > 0)
        def _():
            copy(s, 0, 0, n).wait()

    def tail_copy(t):
        used = pl.multiple_of(tused_ref[t], SEG_ALIGN)
        n = pl.multiple_of(TILE_ROWS - used, SEG_ALIGN)
        return n, pltpu.make_async_copy(ybuf.at[1, pl.ds(0, n), :],
                                        ys_hbm.at[pl.ds(pl.multiple_of(t * TILE_ROWS + used, SEG_ALIGN), n), :],
                                        sem_out.at[1])

    @pl.when(i == 0)
    def _():
        xbuf[...] = jnp.zeros(xbuf.shape, F32)
        ybuf[1] = jnp.zeros(ybuf.shape[1:], F32)
        for_pieces(0, lambda src, dst, n: in_copy(0, src, dst, n).start())

        def start_tail(t, carry):
            n, copy = tail_copy(t)

            @pl.when(n > 0)
            def _():
                copy.start()
            return carry

        def wait_tail(t, carry):
            n, copy = tail_copy(t)

            @pl.when(n > 0)
            def _():
                copy.wait()
            return carry

        lax.fori_loop(0, n_tiles, start_tail, 0)
        lax.fori_loop(0, n_tiles, wait_tail, 0)

    @pl.when(i + 1 < nb)
    def _():
        for_pieces(i + 1, lambda src, dst, n: in_copy(1 - slot, src, dst, n).start())

    new_expert = jnp.logical_or(i == 0, be_ref[i] != be_ref[jnp.maximum(i - 1, 0)])

    @pl.when(jnp.logical_and(new_expert, i < nb))
    def _():
        winbf[...] = win_ref[0].astype(BF16)
        woutbf[...] = wout_ref[0].astype(BF16)

    @pl.when(i < nb)
    def _():
        wait_block(in_copy, i, slot)
        x = xbuf[slot].astype(BF16)
        hmid = jnp.dot(x, winbf[...], preferred_element_type=F32) + bin_ref[0]
        glu = jnp.minimum(hmid[:, :D_FF], SWIGLU_LIMIT)
        lin = jnp.clip(hmid[:, D_FF:], -SWIGLU_LIMIT, SWIGLU_LIMIT)
        act = glu * _sigmoid(SWIGLU_ALPHA * glu) * (lin + 1.0)
        y = jnp.dot(act.astype(BF16), woutbf[...], preferred_element_type=F32) + bout_ref[0]

        @pl.when(i >= 1)
        def _():
            wait_block(out_copy, i - 1, 1 - slot)

        ybuf[slot] = y
        for_pieces(i, lambda src, dst, n: out_copy(slot, src, dst, n).start())

        @pl.when(i == nb - 1)
        def _():
            wait_block(out_copy, i, slot)


def _experts(xs, tables, w_in, b_in, w_out, b_out, n_tiles):
    n_blocks = tables[0].shape[0]

    def wmap(i, be, *_):
        return (be[i], 0, 0)

    grid_spec = pltpu.PrefetchScalarGridSpec(
        num_scalar_prefetch=len(tables),
        grid=(n_blocks,),
        in_specs=[pl.BlockSpec(memory_space=pl.ANY),
                  pl.BlockSpec((1, D, 2 * D_FF), wmap),
                  pl.BlockSpec((1, 1, 2 * D_FF), wmap),
                  pl.BlockSpec((1, D_FF, D), wmap),
                  pl.BlockSpec((1, 1, D), wmap)],
        out_specs=pl.BlockSpec(memory_space=pl.ANY),
        scratch_shapes=[pltpu.VMEM((2, EXPERT_ROWS, D), F32),
                        pltpu.VMEM((2, EXPERT_ROWS, D), F32),
                        pltpu.SemaphoreType.DMA((2,)),
                        pltpu.SemaphoreType.DMA((2,)),
                        pltpu.VMEM((D, 2 * D_FF), BF16),
                        pltpu.VMEM((D_FF, D), BF16)])
    return pl.pallas_call(
        functools.partial(_expert_kernel, n_tiles=n_tiles),
        out_shape=jax.ShapeDtypeStruct(xs.shape, F32),
        grid_spec=grid_spec,
        compiler_params=_cparams(1),
        name="experts",
    )(*tables, xs, w_in, b_in.reshape(N_EXPERTS, 1, 2 * D_FF), w_out, b_out.reshape(N_EXPERTS, 1, D))


def _combine_kernel(ys_ref, pos_ref, gate_ref, x_ref, g_ref, lg_ref, lb_ref, o_ref):
    rows = lax.broadcasted_iota(jnp.int32, (TILE_ROWS, TILE_TOK), 0)
    perm = jnp.zeros((TILE_ROWS, TILE_TOK), F32)
    gmat = jnp.zeros((TILE_ROWS, TILE_TOK), F32)
    for k in range(TOP_K):
        hit = rows == pos_ref[0, k:k + 1, :]
        perm = perm + jnp.where(hit, 1.0, 0.0)
        gmat = gmat + jnp.where(hit, gate_ref[0, k:k + 1, :], 0.0)
    row_gate = jnp.sum(gmat, axis=1, keepdims=True)
    z = ys_ref[...] * row_gate
    hi, mid, lo = _split3(z)
    p = perm.astype(BF16)
    dims = (((0,), (0,)), ((), ()))
    y = (lax.dot_general(p, hi, dims, preferred_element_type=F32)
         + (lax.dot_general(p, mid, dims, preferred_element_type=F32)
            + lax.dot_general(p, lo, dims, preferred_element_type=F32)))
    o_ref[...] = _layer_norm(_ALPHA * x_ref[...] + g_ref[0] * y, lg_ref[...], lb_ref[...])


def _combine(ys, pos, gates, x, mod, ln_g, ln_b, n_prompt, t_lat):
    m = x.shape[0]
    cidx = functools.partial(_cond_index, tm=TILE_TOK, n_prompt=n_prompt, t_lat=t_lat)
    row = pl.BlockSpec((TILE_TOK, D), lambda i: (i, 0))
    vec = pl.BlockSpec((1, D), lambda i: (0, 0))
    sel = pl.BlockSpec((1, TOP_K, TILE_TOK), lambda i: (i, 0, 0))
    return pl.pallas_call(
        _combine_kernel,
        out_shape=jax.ShapeDtypeStruct((m, D), F32),
        grid=(m // TILE_TOK,),
        in_specs=[pl.BlockSpec((TILE_ROWS, D), lambda i: (i, 0)), sel, sel, row,
                  pl.BlockSpec((1, 1, D), lambda i: (cidx(i), 0, 5)), vec, vec],
        out_specs=row,
        compiler_params=_cparams(1),
        name="combine",
    )(ys, pos, gates, x, mod, ln_g.reshape(1, D), ln_b.reshape(1, D))


def _routing_tables(seg_n):
    n_tiles = seg_n.shape[0]
    seg_src = (jnp.arange(n_tiles, dtype=jnp.int32)[:, None] * TILE_ROWS
               + jnp.cumsum(seg_n, axis=1) - seg_n)
    seg_cum = jnp.cumsum(seg_n, axis=0) - seg_n
    count = jnp.sum(seg_n, axis=0)
    padded = (count + EXPERT_ROWS - 1) // EXPERT_ROWS * EXPERT_ROWS
    pad_end = jnp.cumsum(padded)
    pad_start = pad_end - padded
    n_blocks = n_tiles * TILE_ROWS // EXPERT_ROWS + N_EXPERTS
    n_used = pad_end[-1] // EXPERT_ROWS
    blocks = jnp.arange(n_blocks, dtype=jnp.int32)
    owner = jnp.minimum(jnp.sum(blocks[:, None] * EXPERT_ROWS >= pad_end[None, :], axis=1), N_EXPERTS - 1)
    last_owner = jnp.minimum(jnp.sum((n_used - 1) * EXPERT_ROWS >= pad_end), N_EXPERTS - 1)
    block_expert = jnp.where(blocks < n_used, owner, last_owner).astype(jnp.int32)
    own = block_expert[:, None] == jnp.arange(N_EXPERTS, dtype=jnp.int32)[None, :]
    block_off = blocks * EXPERT_ROWS - jnp.sum(jnp.where(own, pad_start[None, :], 0), axis=1)
    block_rows = jnp.clip(jnp.sum(jnp.where(own, count[None, :], 0), axis=1) - block_off, 0, EXPERT_ROWS)
    cum_b = jnp.sum(jnp.where(own[:, None, :], seg_cum[None], 0), axis=2)
    n_b = jnp.sum(jnp.where(own[:, None, :], seg_n[None], 0), axis=2)
    t_lo = jnp.sum(cum_b + n_b <= block_off[:, None], axis=1)
    t_hi = jnp.sum(cum_b < block_off[:, None] + EXPERT_ROWS, axis=1)
    i32 = lambda a: a.astype(jnp.int32)
    return (block_expert, i32(block_off), i32(block_rows), i32(t_lo), i32(t_hi), i32(n_used).reshape(1),
            i32(seg_cum.T.reshape(-1)), i32(seg_n.T.reshape(-1)), i32(seg_src.T.reshape(-1)),
            i32(jnp.sum(seg_n, axis=1)))


def _moe(x, mod, w_router, b_router, w_in, b_in, w_out, b_out, ln_g, ln_b, n_prompt, t_lat):
    xs, pos, gates, cnt = _router(x, mod, w_router, b_router, n_prompt, t_lat)
    tables = _routing_tables(cnt[:, :, 0])
    ys = _experts(xs, tables, w_in, b_in, w_out, b_out, cnt.shape[0])
    return _combine(ys, pos, gates, x, mod, ln_g, ln_b, n_prompt, t_lat)


def kernel(x_prompt, x_sample, cache_k_a, cache_v_a, state_b, cache_k_c, cache_v_c, state_d_c, state_d_n, state_d_m, c, c_ctx, w_mod, b_mod, ln_g, ln_b, w_in_a, q_norm_a, k_norm_a, w_out_a, w_in_b, w_conv_b, w_gate_b, a_log_b, dt_bias_b, o_norm_b, w_out_b, w_in_c, rpb_c, w_out_c, w_in_d, w_gate_d, b_gate_d, o_norm_d, w_out_d, w_router, b_router, w_exp_in, b_exp_in, w_exp_out, b_exp_out):
    bp, t_p, _ = x_prompt.shape
    bs, t_s, _ = x_sample.shape
    past = cache_k_a.shape[2]
    depth = w_mod.shape[0]
    assert depth == 4 and bs + 1 <= N_COND
    n_p, n_s = bp * t_p, bs * t_s
    x = jnp.concatenate([x_prompt.reshape(n_p, D), x_sample.reshape(n_s, D)], axis=0)
    cond = jnp.concatenate([c_ctx.reshape(1, D), c, jnp.zeros((N_COND - 1 - bs, D), F32)], axis=0)
    res = functools.partial(_proj_res_ln, n_prompt=n_p, t_lat=t_s)
    outs = {}
    for i in range(depth):
        kind, j = i % 4, i // 4
        mod = _modulation_table(cond, w_mod[i], b_mod[i]).reshape(N_COND, 1, 6 * D)
        proj = functools.partial(_modmm, x, mod, 1, 0, n_prompt=n_p, t_lat=t_s)
        if kind == 0:
            p = proj(w_in_a[j])
            q, k = _prep_a(p, q_norm_a[j], k_norm_a[j], n_p, t_s)
            hq, hk = HEADS_A * DH_A, KV_A * DH_A
            vcol = hq // hk + 1
            o_p = _attention(q, pl.BlockSpec((t_p, hq), lambda b: (b, 0)),
                             [(k, pl.BlockSpec((t_p, hk), lambda b: (b, 0)),
                               p, pl.BlockSpec((t_p, hk), lambda b: (b, vcol)))],
                             n_p, (bp,), HEADS_A, KV_A, DH_A, pl.BlockSpec((t_p, hq), lambda b: (b, 0)))
            tq = 256
            qb0, lat0 = n_p // tq, n_p // t_s
            kc = cache_k_a[:, j].reshape(bs * past, hk)
            vc = cache_v_a[:, j].reshape(bs * past, hk)
            o_s = _attention(q, pl.BlockSpec((tq, hq), lambda b, t: (qb0 + b * (t_s // tq) + t, 0)),
                             [(kc, pl.BlockSpec((past, hk), lambda b, t: (b, 0)),
                               vc, pl.BlockSpec((past, hk), lambda b, t: (b, 0))),
                              (k, pl.BlockSpec((t_s, hk), lambda b, t: (lat0 + b, 0)),
                               p, pl.BlockSpec((t_s, hk), lambda b, t: (lat0 + b, vcol)))],
                             n_s, (bs, t_s // tq), HEADS_A, KV_A, DH_A,
                             pl.BlockSpec((tq, hq), lambda b, t: (b * (t_s // tq) + t, 0)))
            outs["k_a"] = k[:n_p].reshape(bp, 1, t_p, KV_A, DH_A)
            outs["v_a"] = p[:n_p, hq + hk:].reshape(bp, 1, t_p, KV_A, DH_A)
            x = res("plain", (jnp.concatenate([o_p, o_s], axis=0),), w_out_a[j], x, mod, 2, ln_g[i, 0], ln_b[i, 0])
        elif kind == 1:
            p = proj(w_in_b[j])
            raw = proj(w_gate_b[j])
            zeros8 = jnp.zeros((2, HEADS_B), F32)
            a32 = jnp.concatenate([a_log_b[j], zeros8], axis=1).reshape(1, 32)
            dt32 = jnp.concatenate([dt_bias_b[j], zeros8], axis=1).reshape(1, 32)
            gcol, grow = _gate_layouts(_gates("gdn", raw, a32, dt32))
            qp, kp, vp = _prep_b(p, w_conv_b[j], 0, n_p, t_p)
            qs, ks, vs = _prep_b(p, w_conv_b[j], n_p, n_s, t_s)
            o_p, s_p = _gdn(qp, kp, vp, gcol, grow, None, bp, t_p, 0)
            o_s, _ = _gdn(qs, ks, vs, gcol, grow, state_b[:, j], bs, t_s, n_p // CHUNK)
            outs["s_b"] = s_p.reshape(bp, 1, 2, HEADS_B, DK_B, DK_B)
            o2 = jnp.concatenate([o_p, o_s], axis=1)
            x = res("silu", (o2, p, o_norm_b[j].reshape(1, DK_B)), w_out_b[j], x, mod, 2, ln_g[i, 0], ln_b[i, 0],
                    z_colblock=3)
        elif kind == 2:
            hd = HEADS_C * DH_C
            pq, pk, pv = (proj(w_in_c[j][:, s * hd:(s + 1) * hd]) for s in range(3))
            blk = pl.BlockSpec((t_p, hd), lambda b: (b, 0))
            o_p = _attention(pq, blk, [(pk, blk, pv, blk)], n_p, (bp,), HEADS_C, HEADS_C, DH_C, blk)
            o_s = _nbr_attention(pq, pk, pv, cache_k_c[:, j].reshape(bs * past, hd),
                                 cache_v_c[:, j].reshape(bs * past, hd), rpb_c[j], n_p, bs, t_s, past)
            outs["k_c"] = pk[:n_p].reshape(bp, 1, t_p, HEADS_C, DH_C)
            outs["v_c"] = pv[:n_p].reshape(bp, 1, t_p, HEADS_C, DH_C)
            x = res("plain", (jnp.concatenate([o_p, o_s], axis=0),), w_out_c[j], x, mod, 2, ln_g[i, 0], ln_b[i, 0])
        else:
            p = proj(w_in_d[j])
            raw = proj(w_gate_d[j])
            gcol, grow = _gate_layouts(_gates("mlstm", raw, b_gate_d[j].reshape(1, 32), jnp.zeros((1, 32), F32)))
            h_p, c_p, n_fin, m_fin = _mlstm(p, gcol, grow, None, bp, t_p, 0)
            m0 = jnp.broadcast_to(state_d_m[:, j][..., None], (bs, 2, HEADS_D, 128))
            h_s, _, _, _ = _mlstm(p, gcol, grow, (state_d_c[:, j], state_d_n[:, j], m0), bs, t_s, n_p)
            outs["c_d"] = c_p.reshape(bp, 1, 2, HEADS_D, DK_D, DV_D)
            outs["n_d"] = n_fin.reshape(bp, 1, 2, HEADS_D, DK_D)
            outs["m_d"] = m_fin[..., 0].reshape(bp, 1, 2, HEADS_D)
            h2 = jnp.concatenate([h_p, h_s], axis=1)
            x = res("sigmoid", (h2, p, o_norm_d[j].reshape(1, DV_D)), w_out_d[j], x, mod, 2, ln_g[i, 0], ln_b[i, 0],
                    z_colblock=2)
        x = _moe(x, mod, w_router[i], b_router[i], w_exp_in[i], b_exp_in[i], w_exp_out[i], b_exp_out[i],
                 ln_g[i, 1], ln_b[i, 1], n_p, t_s)
    return (x[:n_p].reshape(bp, t_p, D), x[n_p:].reshape(bs, t_s, D), outs["k_a"], outs["v_a"], outs["s_b"],
            outs["k_c"], outs["v_c"], outs["c_d"], outs["n_d"], outs["m_d"])
```

```python
import functools

import jax
import jax.numpy as jnp
import numpy as np
from jax import lax
from jax.experimental import pallas as pl
from jax.experimental.pallas import tpu as pltpu

F32 = jnp.float32
BF16 = jnp.bfloat16

D = 1024
GRID_W = 64
HEADS_A, KV_A, DH_A = 16, 4, 64
ROPE_THETA = 10000.0
HEADS_B, DK_B = 8, 128
CONV_W = 3
CHUNK = 64
HEADS_C, DH_C = 16, 64
WIN_R, WIN_C = 8, 16
HEADS_D, DK_D, DV_D = 8, 64, 128
N_EXPERTS, TOP_K, D_FF = 32, 4, 1024
SWIGLU_LIMIT, SWIGLU_ALPHA = 7.0, 1.702
NEG_INF = -1e30
N_COND = 8
EXPERT_ROWS = 256
TILE_TOK = 256
SEG_ALIGN = 8
TILE_ROWS = (TILE_TOK * TOP_K + N_EXPERTS * (SEG_ALIGN - 1) + 7) // 8 * 8
VMEM_LIMIT = 56 * 1024 * 1024
_ALPHA = float((2 * 4) ** 0.25)


def _cparams(n_axes):
    return pltpu.CompilerParams(dimension_semantics=("arbitrary",) * n_axes, vmem_limit_bytes=VMEM_LIMIT)


def _dot(a, b):
    return jnp.dot(a.astype(BF16), b.astype(BF16), preferred_element_type=F32)


def _dot_nt(a, b):
    return lax.dot_general(a.astype(BF16), b.astype(BF16), (((1,), (1,)), ((), ())), preferred_element_type=F32)


def _dot_tn(a, b):
    return lax.dot_general(a.astype(BF16), b.astype(BF16), (((0,), (0,)), ((), ())), preferred_element_type=F32)


def _split2(x):
    hi = x.astype(BF16)
    lo = (x - hi.astype(F32)).astype(BF16)
    return hi, lo


def _split3(x):
    hi = x.astype(BF16)
    r = x - hi.astype(F32)
    mid = r.astype(BF16)
    lo = (r - mid.astype(F32)).astype(BF16)
    return hi, mid, lo


def _dot_hi(a, b):
    ah, al = _split2(a)
    bh, bl = _split2(b)
    d = functools.partial(jnp.dot, preferred_element_type=F32)
    return d(ah, bh) + (d(ah, bl) + d(al, bh))


def _mask_dot(mask_bf16, x):
    d = functools.partial(jnp.dot, preferred_element_type=F32)
    hi, mid, lo = _split3(x)
    return d(mask_bf16, hi) + (d(mask_bf16, mid) + d(mask_bf16, lo))


def _dot_mask(x, mask_bf16):
    d = functools.partial(jnp.dot, preferred_element_type=F32)
    hi, mid, lo = _split3(x)
    return d(hi, mask_bf16) + (d(mid, mask_bf16) + d(lo, mask_bf16))


def _layer_norm(x, g, b):
    mu = jnp.mean(x, axis=-1, keepdims=True)
    xc = x - mu
    var = jnp.mean(xc * xc, axis=-1, keepdims=True)
    return xc * lax.rsqrt(var + 1e-5) * g + b


def _sigmoid(x):
    return 1.0 / (1.0 + jnp.exp(-x))


def _cond_index(i, tm, n_prompt, t_lat):
    npt = n_prompt // tm
    return jnp.where(i < npt, 0, 1 + (i - npt) // (t_lat // tm))


def _mod_kernel(c_ref, w_ref, b_ref, o_ref):
    c = c_ref[...]
    o_ref[...] = _dot(c * _sigmoid(c), w_ref[...]) + b_ref[...]


def _modulation_table(cond, w, b, layer):
    n = w.shape[2]
    tn = 1024
    return pl.pallas_call(
        _mod_kernel,
        out_shape=jax.ShapeDtypeStruct((N_COND, n), F32),
        grid=(n // tn,),
        in_specs=[pl.BlockSpec((N_COND, D), lambda j: (0, 0)),
                  pl.BlockSpec((None, D, tn), lambda j: (layer, 0, j)),
                  pl.BlockSpec((1, tn), lambda j: (0, j))],
        out_specs=pl.BlockSpec((N_COND, tn), lambda j: (0, j)),
        compiler_params=_cparams(1),
        name="mod_table",
    )(cond, w, b.reshape(1, n))


def _modmm_kernel(x_ref, sc_ref, sh_ref, w_ref, o_ref, wbf_ref):
    @pl.when(pl.program_id(1) == 0)
    def _():
        wbf_ref[...] = w_ref[...].astype(BF16)

    h = x_ref[...] * (1.0 + sc_ref[0]) + sh_ref[0]
    o_ref[...] = jnp.dot(h.astype(BF16), wbf_ref[...], preferred_element_type=F32)


def _col_tile(n):
    for t in (1024, 512, 256, 128):
        if n % t == 0:
            return t
    return n


def _modmm(x, mod, which_sc, which_sh, w, n_prompt, t_lat, tm=512):
    m = x.shape[0]
    n = w.shape[1]
    tn = _col_tile(n)
    cidx = functools.partial(_cond_index, tm=tm, n_prompt=n_prompt, t_lat=t_lat)
    return pl.pallas_call(
        _modmm_kernel,
        out_shape=jax.ShapeDtypeStruct((m, n), F32),
        grid=(n // tn, m // tm),
        in_specs=[pl.BlockSpec((tm, D), lambda j, i: (i, 0)),
                  pl.BlockSpec((1, 1, D), lambda j, i: (cidx(i), 0, which_sc)),
                  pl.BlockSpec((1, 1, D), lambda j, i: (cidx(i), 0, which_sh)),
                  pl.BlockSpec((D, tn), lambda j, i: (0, j))],
        out_specs=pl.BlockSpec((tm, tn), lambda j, i: (i, j)),
        scratch_shapes=[pltpu.VMEM((D, tn), BF16)],
        compiler_params=_cparams(2),
        name="mod_proj",
    )(x, mod, mod, w)


def _proj_kernel(*refs, kind, prompt_tiles):
    if kind == "plain":
        ap_ref, as_ref, w_ref, x_ref, g_ref, lg_ref, lb_ref, o_ref, wbf_ref = refs
    else:
        (pf_ref, pb_ref, sf_ref, sb_ref, z_ref, on_ref, w_ref, x_ref, g_ref, lg_ref, lb_ref,
         o_ref, wbf_ref, abuf_ref) = refs
    is_prompt = pl.program_id(0) < prompt_tiles

    @pl.when(pl.program_id(0) == 0)
    def _():
        wbf_ref[...] = w_ref[...].astype(BF16)

    if kind == "plain":
        a = jnp.where(is_prompt, ap_ref[...], as_ref[...]).astype(BF16)
    else:
        s = jnp.where(is_prompt, pf_ref[...] + pb_ref[...], sf_ref[...] + sb_ref[...])
        on = on_ref[...]
        hd = on.shape[-1]
        for h in range(D // hd):
            sl = slice(h * hd, (h + 1) * hd)
            sh = s[:, sl]
            y = sh * lax.rsqrt(jnp.mean(sh * sh, axis=-1, keepdims=True) + 1e-6) * on
            z = z_ref[:, sl]
            gate = _sigmoid(z)
            if kind == "silu":
                gate = z * gate
            abuf_ref[:, sl] = (y * gate).astype(BF16)
        a = abuf_ref[...]
    y = jnp.dot(a, wbf_ref[...], preferred_element_type=F32)
    o_ref[...] = _layer_norm(_ALPHA * x_ref[...] + g_ref[0] * y, lg_ref[...], lb_ref[...])


def _proj_res_ln(kind, a_args, w, x, mod, which_g, ln_g, ln_b, n_prompt, t_lat, z_colblock=0, tm=256):
    m = x.shape[0]
    cidx = functools.partial(_cond_index, tm=tm, n_prompt=n_prompt, t_lat=t_lat)
    row = pl.BlockSpec((tm, D), lambda i: (i, 0))
    vec = pl.BlockSpec((1, D), lambda i: (0, 0))
    npt = n_prompt // tm
    prompt_rows = pl.BlockSpec((tm, D), lambda i: (jnp.minimum(i, npt - 1), 0))
    latent_rows = pl.BlockSpec((tm, D), lambda i: (jnp.maximum(i - npt, 0), 0))
    if kind == "plain":
        a_specs = [prompt_rows, latent_rows]
        scratch = [pltpu.VMEM((D, D), BF16)]
    else:
        on = a_args[5]
        a_specs = [prompt_rows, prompt_rows, latent_rows, latent_rows,
                   pl.BlockSpec((tm, D), lambda i: (i, z_colblock)),
                   pl.BlockSpec((1, on.shape[-1]), lambda i: (0, 0))]
        scratch = [pltpu.VMEM((D, D), BF16), pltpu.VMEM((tm, D), BF16)]
    return pl.pallas_call(
        functools.partial(_proj_kernel, kind=kind, prompt_tiles=npt),
        out_shape=jax.ShapeDtypeStruct((m, D), F32),
        grid=(m // tm,),
        in_specs=a_specs + [pl.BlockSpec((D, D), lambda i: (0, 0)), row,
                            pl.BlockSpec((1, 1, D), lambda i: (cidx(i), 0, which_g)), vec, vec],
        out_specs=row,
        scratch_shapes=scratch,
        compiler_params=_cparams(1),
        name="out_proj_" + kind,
    )(*a_args, w, x, mod, ln_g.reshape(1, D), ln_b.reshape(1, D))


def _prep_a_kernel(p_ref, qn_ref, kn_ref, cos_ref, sin_ref, q_ref, k_ref):
    cos = cos_ref[...]
    sin = sin_ref[...]
    half = DH_A // 2

    def one(x, g):
        y = x * lax.rsqrt(jnp.mean(x * x, axis=-1, keepdims=True) + 1e-6) * g
        swapped = jnp.concatenate([y[:, half:], y[:, :half]], axis=-1)
        return y * cos + swapped * sin

    for h in range(HEADS_A):
        sl = slice(h * DH_A, (h + 1) * DH_A)
        q_ref[:, sl] = one(p_ref[:, sl], qn_ref[...])
    for h in range(KV_A):
        sl = slice(h * DH_A, (h + 1) * DH_A)
        k_ref[:, sl] = one(p_ref[:, HEADS_A * DH_A + h * DH_A: HEADS_A * DH_A + (h + 1) * DH_A], kn_ref[...])


def _rope_tables(t_lat, tm):
    quarter = DH_A // 4
    t = np.arange(t_lat)
    inv_freq = ROPE_THETA ** (-np.arange(quarter, dtype=np.float32) / quarter)
    ang = np.concatenate([(t // GRID_W)[:, None] * inv_freq, (t % GRID_W)[:, None] * inv_freq], axis=-1)
    ang = jnp.asarray(ang, F32)
    cos, sin = jnp.cos(ang), jnp.sin(ang)
    cos64 = jnp.concatenate([cos, cos], axis=-1)
    sin64 = jnp.concatenate([-sin, sin], axis=-1)
    cos64 = jnp.concatenate([jnp.ones((tm, DH_A), F32), cos64], axis=0)
    sin64 = jnp.concatenate([jnp.zeros((tm, DH_A), F32), sin64], axis=0)
    return cos64, sin64


def _prep_a(p, q_norm, k_norm, n_prompt, t_lat, tm=256):
    m = p.shape[0]
    cos64, sin64 = _rope_tables(t_lat, tm)
    npt = n_prompt // tm

    def tab(i):
        return (jnp.where(i < npt, 0, 1 + (i - npt) % (t_lat // tm)), 0)

    return pl.pallas_call(
        _prep_a_kernel,
        out_shape=(jax.ShapeDtypeStruct((m, HEADS_A * DH_A), F32), jax.ShapeDtypeStruct((m, KV_A * DH_A), F32)),
        grid=(m // tm,),
        in_specs=[pl.BlockSpec((tm, p.shape[1]), lambda i: (i, 0)),
                  pl.BlockSpec((1, DH_A), lambda i: (0, 0)),
                  pl.BlockSpec((1, DH_A), lambda i: (0, 0)),
                  pl.BlockSpec((tm, DH_A), tab),
                  pl.BlockSpec((tm, DH_A), tab)],
        out_specs=(pl.BlockSpec((tm, HEADS_A * DH_A), lambda i: (i, 0)),
                   pl.BlockSpec((tm, KV_A * DH_A), lambda i: (i, 0))),
        compiler_params=_cparams(1),
        name="prep_a",
    )(p, q_norm.reshape(1, DH_A), k_norm.reshape(1, DH_A), cos64, sin64)


def _softmax_pv(scores, values):
    m = scores[0].max(axis=-1, keepdims=True)
    for s in scores[1:]:
        m = jnp.maximum(m, s.max(axis=-1, keepdims=True))
    ps = [jnp.exp(s - m) for s in scores]
    den = ps[0].sum(axis=-1, keepdims=True)
    for p in ps[1:]:
        den = den + p.sum(axis=-1, keepdims=True)
    inv = 1.0 / den
    o = _dot(ps[0] * inv, values[0])
    for p, v in zip(ps[1:], values[1:]):
        o = o + _dot(p * inv, v)
    return o


def _pipelined_heads(n_heads, score_fn, finish_fn, batch=4):
    pending = []
    for h0 in range(0, n_heads, batch):
        current = [(h, score_fn(h)) for h in range(h0, min(h0 + batch, n_heads))]
        for h, sc in pending:
            finish_fn(h, sc)
        pending = current
    for h, sc in pending:
        finish_fn(h, sc)


def _attn_kernel(*refs, heads, kv_heads, dh, nseg):
    q_ref = refs[0]
    kv = refs[1:1 + 2 * nseg]
    o_ref = refs[1 + 2 * nseg]
    scale = dh ** -0.5
    group = heads // kv_heads

    def scores(h):
        ksl = slice(h // group * dh, (h // group + 1) * dh)
        q = q_ref[:, h * dh:(h + 1) * dh]
        return [_dot_nt(q, kv[2 * s][:, ksl]) * scale for s in range(nseg)]

    def finish(h, sc):
        ksl = slice(h // group * dh, (h // group + 1) * dh)
        o_ref[:, h * dh:(h + 1) * dh] = _softmax_pv(sc, [kv[2 * s + 1][:, ksl] for s in range(nseg)])

    _pipelined_heads(heads, scores, finish)


def _attention(q_arr, q_spec, segs, out_rows, grid, heads, kv_heads, dh, out_spec):
    in_specs = [q_spec]
    args = [q_arr]
    for (k_arr, k_spec, v_arr, v_spec) in segs:
        in_specs += [k_spec, v_spec]
        args += [k_arr, v_arr]
    return pl.pallas_call(
        functools.partial(_attn_kernel, heads=heads, kv_heads=kv_heads, dh=dh, nseg=len(segs)),
        out_shape=jax.ShapeDtypeStruct((out_rows, heads * dh), F32),
        grid=grid,
        in_specs=in_specs,
        out_specs=out_spec,
        compiler_params=_cparams(len(grid)),
        name="attention",
    )(*args)


def _window_start(r, n_rows):
    wr = min(WIN_R, n_rows)
    return jnp.minimum(jnp.maximum(r - wr // 2, 0), n_rows - wr)


def _nbr_kernel(q_ref, kl_ref, vl_ref, kc_ref, vc_ref, bias_ref, o_ref, *, n_rows):
    r = pl.program_id(1)
    wr = min(WIN_R, n_rows)
    start = pl.multiple_of(_window_start(r, n_rows) * GRID_W, GRID_W)
    scale = DH_C ** -0.5
    window = pl.ds(start, wr * GRID_W)

    def scores(h):
        sl = slice(h * DH_C, (h + 1) * DH_C)
        q = q_ref[:, sl]
        return [_dot_nt(q, kl_ref[window, sl]) * scale + bias_ref[0, h], _dot_nt(q, kc_ref[:, sl]) * scale]

    def finish(h, sc):
        sl = slice(h * DH_C, (h + 1) * DH_C)
        o_ref[:, sl] = _softmax_pv(sc, [vl_ref[window, sl], vc_ref[:, sl]])

    _pipelined_heads(HEADS_C, scores, finish)


def _nbr_bias(rpb, n_rows):
    wr = min(WIN_R, n_rows)
    cols = np.arange(GRID_W)
    col_start = np.clip(cols - WIN_C // 2, 0, GRID_W - WIN_C)
    col_ok = (cols[None, :] >= col_start[:, None]) & (cols[None, :] < col_start[:, None] + WIN_C)
    dc = np.clip(cols[None, :] - cols[:, None], -(WIN_C - 1), WIN_C - 1) + (WIN_C - 1)
    n_off = 2 * WIN_R - wr
    pick = np.zeros((2 * WIN_C - 1, GRID_W * GRID_W), np.float32)
    pick[dc.reshape(-1), np.arange(GRID_W * GRID_W)] = 1.0
    tab = jnp.einsum("hrd,dx->hrx", rpb.astype(F32), jnp.asarray(pick), precision=lax.Precision.HIGHEST)
    tab = jnp.where(col_ok[None, None], tab.reshape(HEADS_C, 2 * WIN_R - 1, GRID_W, GRID_W), NEG_INF)
    b = jnp.stack([tab[:, off:off + wr] for off in range(n_off)], axis=0)
    return jnp.transpose(b, (0, 1, 3, 2, 4)).reshape(n_off, HEADS_C, GRID_W, wr * GRID_W)


def _nbr_attention(q, k, v, k_ctx, v_ctx, rpb, n_prompt, n_lat_b, t_lat, past):
    n_rows = t_lat // GRID_W
    wr = min(WIN_R, n_rows)
    bias = _nbr_bias(rpb, n_rows)
    row0 = n_prompt // GRID_W
    lat0 = n_prompt // t_lat
    hd = HEADS_C * DH_C

    def off(b, r):
        return (_window_start(r, n_rows) - r + (WIN_R - 1), 0, 0, 0)

    return pl.pallas_call(
        functools.partial(_nbr_kernel, n_rows=n_rows),
        out_shape=jax.ShapeDtypeStruct((n_lat_b * t_lat, hd), F32),
        grid=(n_lat_b, n_rows),
        in_specs=[pl.BlockSpec((GRID_W, hd), lambda b, r: (row0 + b * n_rows + r, 0)),
                  pl.BlockSpec((t_lat, hd), lambda b, r: (lat0 + b, 0)),
                  pl.BlockSpec((t_lat, hd), lambda b, r: (lat0 + b, 0)),
                  pl.BlockSpec((past, hd), lambda b, r: (b, 0)),
                  pl.BlockSpec((past, hd), lambda b, r: (b, 0)),
                  pl.BlockSpec((1, HEADS_C, GRID_W, wr * GRID_W), off)],
        out_specs=pl.BlockSpec((GRID_W, hd), lambda b, r: (b * n_rows + r, 0)),
        compiler_params=_cparams(2),
        name="nbr_attention",
    )(q, k, v, k_ctx, v_ctx, bias)


def _chunk_masks(d):
    r = lax.broadcasted_iota(jnp.int32, (CHUNK, CHUNK), 0)
    c = lax.broadcasted_iota(jnp.int32, (CHUNK, CHUNK), 1)
    diff = r - c if d == 0 else c - r
    incl = diff >= 0
    strict = diff > 0
    incl_t = diff <= 0
    return incl, strict, incl_t


def _gate_kernel(raw_ref, p0_ref, p1_ref, o_ref, *, kind):
    raw = raw_ref[...]
    lane = lax.broadcasted_iota(jnp.int32, raw.shape, 1)
    first = (lane % 16) < 8
    if kind == "gdn":
        g = -jnp.exp(p0_ref[...]) * jax.nn.softplus(raw + p1_ref[...])
        o_ref[...] = jnp.where(first, g, _sigmoid(raw))
    else:
        x = raw + p0_ref[...]
        o_ref[...] = jnp.where(first, x, -jax.nn.softplus(-x))


def _gates(kind, raw, p0, p1, tm=512):
    m, n = raw.shape
    return pl.pallas_call(
        functools.partial(_gate_kernel, kind=kind),
        out_shape=jax.ShapeDtypeStruct((m, n), F32),
        grid=(m // tm,),
        in_specs=[pl.BlockSpec((tm, n), lambda i: (i, 0)),
                  pl.BlockSpec((1, n), lambda i: (0, 0)),
                  pl.BlockSpec((1, n), lambda i: (0, 0))],
        out_specs=pl.BlockSpec((tm, n), lambda i: (i, 0)),
        compiler_params=_cparams(1),
        name="gates_" + kind,
    )(raw, p0, p1)


def _gate_layouts(g):
    m = g.shape[0]
    col = g.reshape(m // CHUNK, CHUNK, 2, 16).transpose(2, 0, 1, 3)
    return col, col.transpose(0, 1, 3, 2)


def _prep_b_kernel(p_ref, prev_ref, next_ref, w_ref, q_ref, k_ref, v_ref, *, tm, tiles_per_seq):
    i = pl.program_id(0)
    pos = i % tiles_per_seq
    rows = lax.broadcasted_iota(jnp.int32, (tm, D), 0)
    outs = (q_ref, k_ref, v_ref)
    for seg in range(3):
        sl = slice(seg * D, (seg + 1) * D)
        x = p_ref[:, sl]
        prev_row = jnp.where(pos == 0, 0.0, prev_ref[7:8, sl])
        next_row = jnp.where(pos == tiles_per_seq - 1, 0.0, next_ref[0:1, sl])
        x_prev = jnp.where(rows == 0, prev_row, pltpu.roll(x, 1, axis=0))
        x_next = jnp.where(rows == tm - 1, next_row, pltpu.roll(x, tm - 1, axis=0))
        y = x_prev * w_ref[0:1, sl] + x * w_ref[1:2, sl] + x_next * w_ref[2:3, sl]
        y = y * _sigmoid(y)
        if seg < 2:
            for h in range(HEADS_B):
                hs = slice(h * DK_B, (h + 1) * DK_B)
                yh = y[:, hs]
                outs[seg][:, hs] = yh * lax.rsqrt(jnp.sum(yh * yh, axis=-1, keepdims=True) + 1e-6)
        else:
            v_ref[...] = y


def _prep_b(p, w_conv, row0, n_rows, t_seq, tm=256):
    total8 = p.shape[0] // 8
    r0 = row0 // tm
    tiles_per_seq = t_seq // tm
    out = jax.ShapeDtypeStruct((n_rows, D), F32)
    ospec = pl.BlockSpec((tm, D), lambda i: (i, 0))
    return pl.pallas_call(
        functools.partial(_prep_b_kernel, tm=tm, tiles_per_seq=tiles_per_seq),
        out_shape=(out, out, out),
        grid=(n_rows // tm,),
        in_specs=[pl.BlockSpec((tm, p.shape[1]), lambda i: (r0 + i, 0)),
                  pl.BlockSpec((8, p.shape[1]), lambda i: (jnp.maximum((r0 + i) * (tm // 8) - 1, 0), 0)),
                  pl.BlockSpec((8, p.shape[1]), lambda i: (jnp.minimum((r0 + i + 1) * (tm // 8), total8 - 1), 0)),
                  pl.BlockSpec((CONV_W, 3 * D), lambda i: (0, 0))],
        out_specs=(ospec, ospec, ospec),
        compiler_params=_cparams(1),
        name="prep_b",
    )(p, p, p, w_conv)


def _unit_tri_inverses(mats):
    r = lax.broadcasted_iota(jnp.int32, mats[0].shape, 0)
    c = lax.broadcasted_iota(jnp.int32, mats[0].shape, 1)
    eye = jnp.where(r == c, 1.0, 0.0)
    ps = [-a for a in mats]
    ts = [eye + p for p in ps]
    for _ in range(int(np.log2(CHUNK)) - 1):
        ps = [_dot_hi(p, p) for p in ps]
        ts = [t + _dot_hi(t, p) for t, p in zip(ts, ps)]
    return ts


def _gdn_kernel(*refs, n_chunks, zero_init):
    ins, rest = refs[:10], refs[10:]
    if not zero_init:
        s0_ref, rest = rest[0], rest[1:]
    o_refs, (sfin_ref, s_ref) = rest[:2], rest[2:]
    c = pl.program_id(1)

    @pl.when(c == 0)
    def _():
        if zero_init:
            s_ref[...] = jnp.zeros(s_ref.shape, F32)
        else:
            s_ref[...] = s0_ref[0]

    scale = DK_B ** -0.5
    qh, kh, vh, gc_c, b_c, gl, states, decay, strict = [], [], [], [], [], [], [], [], []
    for d in range(2):
        q_ref, k_ref, v_ref, gc_ref, gr_ref = ins[5 * d:5 * d + 5]
        incl, strict_d, incl_t = _chunk_masks(d)
        gcol = gc_ref[0, 0]
        grow = gr_ref[0, 0]
        gcs = _mask_dot(jnp.where(incl, 1.0, 0.0).astype(BF16), gcol)
        grs = _dot_mask(grow, jnp.where(incl_t, 1.0, 0.0).astype(BF16))
        last = CHUNK - 1 if d == 0 else 0
        q_all, k_all, v_all = q_ref[...], k_ref[...], v_ref[...]
        for h in range(HEADS_B):
            hs = slice(h * DK_B, (h + 1) * DK_B)
            qh.append(q_all[:, hs] * scale)
            kh.append(k_all[:, hs])
            vh.append(v_all[:, hs])
            gc_c.append(gcs[:, h:h + 1])
            b_c.append(gcol[:, HEADS_B + h:HEADS_B + h + 1])
            gl.append(gcs[last:last + 1, h:h + 1])
            states.append(s_ref[d, h])
            decay.append(jnp.exp(jnp.where(incl, gcs[:, h:h + 1] - grs[h:h + 1, :], NEG_INF)))
            strict.append(strict_d)
    chains = range(2 * HEADS_B)
    egc = [jnp.exp(g) for g in gc_c]
    kb = [kh[i] * b_c[i] for i in chains]
    a_mat = [jnp.where(strict[i], _dot_nt(kb[i], kh[i]) * decay[i], 0.0) for i in chains]
    attn = [_dot_nt(qh[i], kh[i]) * decay[i] for i in chains]
    o_state = [_dot(qh[i] * egc[i], states[i]) for i in chains]
    t_inv = _unit_tri_inverses(a_mat)
    u = [_dot_hi(t_inv[i], vh[i] * b_c[i]) for i in chains]
    w = [_dot_hi(t_inv[i], kb[i] * egc[i]) for i in chains]
    v_new = [u[i] - _dot(w[i], states[i]) for i in chains]
    outs = [o_state[i] + _dot(attn[i], v_new[i]) for i in chains]
    for d in range(2):
        o_refs[d][...] = jnp.concatenate(outs[d * HEADS_B:(d + 1) * HEADS_B], axis=1)
    for i in chains:
        s_ref[i // HEADS_B, i % HEADS_B] = (states[i] * jnp.exp(gl[i])
                                            + _dot_tn(kh[i] * jnp.exp(gl[i] - gc_c[i]), v_new[i]))

    @pl.when(c == n_chunks - 1)
    def _():
        sfin_ref[0] = s_ref[...]


def _chunk_index(d, c, n_chunks):
    return c + d * (n_chunks - 1 - 2 * c)


def _gdn(q, k, v, gcol, grow, s0, n_seq, t_seq, chunk0):
    n_chunks = t_seq // CHUNK
    in_specs, args, out_specs = [], [], []
    for d in range(2):
        def rows(b, c, d=d):
            return (b * n_chunks + _chunk_index(d, c, n_chunks), 0)

        def gidx(b, c, d=d):
            return (d, chunk0 + b * n_chunks + _chunk_index(d, c, n_chunks), 0, 0)

        blk = pl.BlockSpec((CHUNK, D), rows)
        in_specs += [blk, blk, blk, pl.BlockSpec((1, 1, CHUNK, 16), gidx), pl.BlockSpec((1, 1, 16, CHUNK), gidx)]
        args += [q, k, v, gcol, grow]
        out_specs.append(blk)
    state_spec = pl.BlockSpec((1, 2, HEADS_B, DK_B, DK_B), lambda b, c: (b, 0, 0, 0, 0))
    if s0 is not None:
        in_specs.append(state_spec)
        args.append(s0)
    m = n_seq * t_seq
    return pl.pallas_call(
        functools.partial(_gdn_kernel, n_chunks=n_chunks, zero_init=s0 is None),
        out_shape=(jax.ShapeDtypeStruct((m, D), F32), jax.ShapeDtypeStruct((m, D), F32),
                   jax.ShapeDtypeStruct((n_seq, 2, HEADS_B, DK_B, DK_B), F32)),
        grid=(n_seq, n_chunks),
        in_specs=in_specs,
        out_specs=tuple(out_specs) + (state_spec,),
        scratch_shapes=[pltpu.VMEM((2, HEADS_B, DK_B, DK_B), F32)],
        compiler_params=_cparams(2),
        name="gdn",
    )(*args)


def _mlstm_kernel(*refs, n_chunks, zero_init):
    ins, rest = refs[:8], refs[8:]
    if not zero_init:
        (c0_ref, n0_ref, m0_ref), rest = rest[:3], rest[3:]
    o_refs, (cfin_ref, nfin_ref, mfin_ref, c_ref, n_ref, m_ref) = rest[:2], rest[2:]
    c = pl.program_id(1)

    @pl.when(c == 0)
    def _():
        if zero_init:
            c_ref[...] = jnp.zeros(c_ref.shape, F32)
            n_ref[...] = jnp.zeros(n_ref.shape, F32)
            m_ref[...] = jnp.zeros(m_ref.shape, F32)
        else:
            c_ref[...] = c0_ref[0]
            n_ref[...] = n0_ref[0]
            m_ref[...] = m0_ref[0]

    scale = DK_D ** -0.5
    qh, kh, vh, b_c, ig_c, m_old, n_old, c_old, d_intra, b_last = [], [], [], [], [], [], [], [], [], []
    for d in range(2):
        qk_ref, v_ref, gc_ref, gr_ref = ins[4 * d:4 * d + 4]
        incl, _, incl_t = _chunk_masks(d)
        gcol = gc_ref[0, 0]
        grow = gr_ref[0, 0]
        bcs = _mask_dot(jnp.where(incl, 1.0, 0.0).astype(BF16), gcol)
        brs = _dot_mask(grow, jnp.where(incl_t, 1.0, 0.0).astype(BF16))
        last = CHUNK - 1 if d == 0 else 0
        qk_all, v_all = qk_ref[...], v_ref[...]
        m_all, n_all = m_ref[d], n_ref[d]
        for h in range(HEADS_D):
            qh.append(qk_all[:, h * DK_D:(h + 1) * DK_D] * scale)
            kh.append(qk_all[:, HEADS_D * DK_D + h * DK_D: HEADS_D * DK_D + (h + 1) * DK_D])
            vh.append(v_all[:, h * DV_D:(h + 1) * DV_D])
            b_c.append(bcs[:, HEADS_D + h:HEADS_D + h + 1])
            ig_c.append(gcol[:, h:h + 1])
            m_old.append(m_all[h:h + 1, 0:1])
            n_old.append(n_all[h:h + 1, :])
            c_old.append(c_ref[d, h])
            d_intra.append(jnp.where(incl, bcs[:, HEADS_D + h:HEADS_D + h + 1]
                                     - brs[HEADS_D + h:HEADS_D + h + 1, :] + grow[h:h + 1, :], NEG_INF))
            b_last.append(bcs[last:last + 1, HEADS_D + h:HEADS_D + h + 1])
    heads = range(2 * HEADS_D)
    a_inter = [b_c[h] + m_old[h] for h in heads]
    m_t = [jnp.maximum(a_inter[h], jnp.max(d_intra[h], axis=-1, keepdims=True)) for h in heads]
    w_inter = [jnp.exp(a_inter[h] - m_t[h]) for h in heads]
    s = [_dot_nt(qh[h], kh[h]) * jnp.exp(d_intra[h] - m_t[h]) for h in heads]
    q_state = [_dot(qh[h], c_old[h]) for h in heads]
    d_state = [b_last[h] - b_c[h] + ig_c[h] for h in heads]
    m_new = [jnp.maximum(b_last[h] + m_old[h], jnp.max(d_state[h], axis=0, keepdims=True)) for h in heads]
    w_old = [jnp.exp(b_last[h] + m_old[h] - m_new[h]) for h in heads]
    kw = [kh[h] * jnp.exp(d_state[h] - m_new[h]) for h in heads]
    kv = [_dot_tn(kw[h], vh[h]) for h in heads]
    outs = []
    for h in heads:
        num = w_inter[h] * q_state[h] + _dot(s[h], vh[h])
        qn = jnp.sum(qh[h] * n_old[h], axis=-1, keepdims=True)
        den = w_inter[h] * qn + jnp.sum(s[h], axis=-1, keepdims=True)
        outs.append(num / jnp.maximum(jnp.abs(den), jnp.exp(-m_t[h])))
    for h in heads:
        c_ref[h // HEADS_D, h % HEADS_D] = w_old[h] * c_old[h] + kv[h]
    for d in range(2):
        hd = range(d * HEADS_D, (d + 1) * HEADS_D)
        o_refs[d][...] = jnp.concatenate([outs[h] for h in hd], axis=1)
        n_ref[d] = jnp.concatenate([w_old[h] * n_old[h] + jnp.sum(kw[h], axis=0, keepdims=True) for h in hd], axis=0)
        m_ref[d] = jnp.concatenate([jnp.broadcast_to(m_new[h], (1, m_ref.shape[2])) for h in hd], axis=0)

    @pl.when(c == n_chunks - 1)
    def _():
        cfin_ref[0] = c_ref[...]
        nfin_ref[0] = n_ref[...]
        mfin_ref[0] = m_ref[...]


def _mlstm(p, gcol, grow, states, n_seq, t_seq, row0):
    n_chunks = t_seq // CHUNK
    chunk0 = row0 // CHUNK

    in_specs, args, out_specs = [], [], []
    for d in range(2):
        def rows(b, c, d=d):
            return chunk0 + b * n_chunks + _chunk_index(d, c, n_chunks)

        def gidx(b, c, d=d, rows=rows):
            return (d, rows(b, c), 0, 0)

        in_specs += [pl.BlockSpec((CHUNK, D), lambda b, c, rows=rows: (rows(b, c), 0)),
                     pl.BlockSpec((CHUNK, D), lambda b, c, rows=rows: (rows(b, c), 1)),
                     pl.BlockSpec((1, 1, CHUNK, 16), gidx),
                     pl.BlockSpec((1, 1, 16, CHUNK), gidx)]
        args += [p, p, gcol, grow]
        out_specs.append(pl.BlockSpec((CHUNK, D), lambda b, c, d=d: (b * n_chunks + _chunk_index(d, c, n_chunks), 0)))
    c_spec = pl.BlockSpec((1, 2, HEADS_D, DK_D, DV_D), lambda b, c: (b, 0, 0, 0, 0))
    n_spec = pl.BlockSpec((1, 2, HEADS_D, DK_D), lambda b, c: (b, 0, 0, 0))
    m_spec = pl.BlockSpec((1, 2, HEADS_D, 128), lambda b, c: (b, 0, 0, 0))
    if states is not None:
        in_specs += [c_spec, n_spec, m_spec]
        args += list(states)
    m = n_seq * t_seq
    return pl.pallas_call(
        functools.partial(_mlstm_kernel, n_chunks=n_chunks, zero_init=states is None),
        out_shape=(jax.ShapeDtypeStruct((m, D), F32), jax.ShapeDtypeStruct((m, D), F32),
                   jax.ShapeDtypeStruct((n_seq, 2, HEADS_D, DK_D, DV_D), F32),
                   jax.ShapeDtypeStruct((n_seq, 2, HEADS_D, DK_D), F32),
                   jax.ShapeDtypeStruct((n_seq, 2, HEADS_D, 128), F32)),
        grid=(n_seq, n_chunks),
        in_specs=in_specs,
        out_specs=tuple(out_specs) + (c_spec, n_spec, m_spec),
        scratch_shapes=[pltpu.VMEM((2, HEADS_D, DK_D, DV_D), F32),
                        pltpu.VMEM((2, HEADS_D, DK_D), F32),
                        pltpu.VMEM((2, HEADS_D, 128), F32)],
        compiler_params=_cparams(2),
        name="mlstm",
    )(*args)


def _router_kernel(x_ref, sc_ref, sh_ref, wt_ref, b_ref, xs_ref, pos_ref, gate_ref, cnt_ref):
    h = (x_ref[...] * (1.0 + sc_ref[0]) + sh_ref[0]).astype(BF16)
    logits = _dot_nt(wt_ref[...], h) + b_ref[...]
    sub = lax.broadcasted_iota(jnp.int32, logits.shape, 0)
    onehots, vals = [], []
    for k in range(TOP_K):
        m = jnp.max(logits, axis=0, keepdims=True)
        idx = jnp.min(jnp.where(logits == m, sub, N_EXPERTS), axis=0, keepdims=True)
        hit = sub == idx
        logits = jnp.where(hit, -jnp.inf, logits)
        onehots.append(jnp.where(hit, 1.0, 0.0))
        vals.append(m)
    exps = [jnp.exp(v - vals[0]) for v in vals]
    den = exps[0] + exps[1] + exps[2] + exps[3]
    counts = [jnp.sum(o, axis=1, keepdims=True) for o in onehots]
    n = counts[0] + counts[1] + counts[2] + counts[3]
    n_al = jnp.floor((n + (SEG_ALIGN - 1)) * (1.0 / SEG_ALIGN)) * SEG_ALIGN
    er = lax.broadcasted_iota(jnp.int32, (N_EXPERTS, N_EXPERTS), 0)
    ec = lax.broadcasted_iota(jnp.int32, (N_EXPERTS, N_EXPERTS), 1)
    before_e = jnp.where(ec < er, 1.0, 0.0).astype(BF16)
    start = jnp.dot(before_e, jnp.broadcast_to(n_al, logits.shape).astype(BF16), preferred_element_type=F32)
    tr = lax.broadcasted_iota(jnp.int32, (TILE_TOK, TILE_TOK), 0)
    tc = lax.broadcasted_iota(jnp.int32, (TILE_TOK, TILE_TOK), 1)
    before_t = jnp.where(tr < tc, 1.0, 0.0).astype(BF16)
    rows = lax.broadcasted_iota(jnp.int32, (TILE_ROWS, TILE_TOK), 0)
    perm = jnp.zeros((TILE_ROWS, TILE_TOK), F32)
    prev = jnp.zeros_like(n)
    for k in range(TOP_K):
        cum = jnp.dot(onehots[k].astype(BF16), before_t, preferred_element_type=F32)
        pos = jnp.sum(onehots[k] * (start + prev + cum), axis=0, keepdims=True).astype(jnp.int32)
        prev = prev + counts[k]
        pos_ref[0, k:k + 1, :] = pos
        gate_ref[0, k:k + 1, :] = exps[k] / den
        perm = perm + jnp.where(rows == pos, 1.0, 0.0)
    xs_ref[...] = jnp.dot(perm.astype(BF16), h, preferred_element_type=F32)
    cnt_ref[0] = jnp.broadcast_to(n_al, cnt_ref.shape[1:]).astype(jnp.int32)


def _router(x, mod, w, b, n_prompt, t_lat):
    m = x.shape[0]
    nt = m // TILE_TOK
    cidx = functools.partial(_cond_index, tm=TILE_TOK, n_prompt=n_prompt, t_lat=t_lat)
    sel = pl.BlockSpec((1, TOP_K, TILE_TOK), lambda i: (i, 0, 0))
    return pl.pallas_call(
        _router_kernel,
        out_shape=(jax.ShapeDtypeStruct((nt * TILE_ROWS, D), F32),
                   jax.ShapeDtypeStruct((nt, TOP_K, TILE_TOK), jnp.int32),
                   jax.ShapeDtypeStruct((nt, TOP_K, TILE_TOK), F32),
                   jax.ShapeDtypeStruct((nt, N_EXPERTS, 128), jnp.int32)),
        grid=(nt,),
        in_specs=[pl.BlockSpec((TILE_TOK, D), lambda i: (i, 0)),
                  pl.BlockSpec((1, 1, D), lambda i: (cidx(i), 0, 4)),
                  pl.BlockSpec((1, 1, D), lambda i: (cidx(i), 0, 3)),
                  pl.BlockSpec((N_EXPERTS, D), lambda i: (0, 0)),
                  pl.BlockSpec((N_EXPERTS, 1), lambda i: (0, 0))],
        out_specs=(pl.BlockSpec((TILE_ROWS, D), lambda i: (i, 0)), sel, sel,
                   pl.BlockSpec((1, N_EXPERTS, 128), lambda i: (i, 0, 0))),
        compiler_params=_cparams(1),
        name="router",
    )(x, mod, mod, w.T, b.reshape(N_EXPERTS, 1))


def _expert_kernel(be_ref, boff_ref, brows_ref, tlo_ref, thi_ref, nb_ref, scum_ref, sn_ref, ssrc_ref, tused_ref,
                   xs_hbm, win_ref, bin_ref, wout_ref, bout_ref, ys_hbm,
                   xbuf, ybuf, sem_in, sem_out, winbf, woutbf, *, n_tiles):
    i = pl.program_id(0)
    nb = nb_ref[0]
    slot = i % 2

    def for_pieces(b, fn):
        e = be_ref[b]
        off = boff_ref[b]

        def body(t, carry):
            j = e * n_tiles + t
            s0 = scum_ref[j]
            lo = jnp.maximum(s0, off)
            hi = jnp.minimum(s0 + sn_ref[j], off + EXPERT_ROWS)

            @pl.when(hi > lo)
            def _():
                fn(pl.multiple_of(ssrc_ref[j] + (lo - s0), SEG_ALIGN), pl.multiple_of(lo - off, SEG_ALIGN),
                   pl.multiple_of(hi - lo, SEG_ALIGN))
            return carry
        lax.fori_loop(tlo_ref[b], thi_ref[b], body, 0)

    def in_copy(s, src, dst, n):
        return pltpu.make_async_copy(xs_hbm.at[pl.ds(src, n), :], xbuf.at[s, pl.ds(dst, n), :], sem_in.at[s])

    def out_copy(s, src, dst, n):
        return pltpu.make_async_copy(ybuf.at[s, pl.ds(dst, n), :], ys_hbm.at[pl.ds(src, n), :], sem_out.at[s])

    def wait_block(copy, b, s):
        n = pl.multiple_of(brows_ref[b], SEG_ALIGN)

        @pl.when(n > 0)
        def _():
            copy(s, 0, 0, n).wait()

    def tail_copy(t):
        used = pl.multiple_of(tused_ref[t], SEG_ALIGN)
        n = pl.multiple_of(TILE_ROWS - used, SEG_ALIGN)
        return n, pltpu.make_async_copy(ybuf.at[1, pl.ds(0, n), :],
                                        ys_hbm.at[pl.ds(pl.multiple_of(t * TILE_ROWS + used, SEG_ALIGN), n), :],
                                        sem_out.at[1])

    @pl.when(i == 0)
    def _():
        xbuf[...] = jnp.zeros(xbuf.shape, F32)
        ybuf[1] = jnp.zeros(ybuf.shape[1:], F32)
        for_pieces(0, lambda src, dst, n: in_copy(0, src, dst, n).start())

        def start_tail(t, carry):
            n, copy = tail_copy(t)

            @pl.when(n > 0)
            def _():
                copy.start()
            return carry

        def wait_tail(t, carry):
            n, copy = tail_copy(t)

            @pl.when(n > 0)
            def _():
                copy.wait()
            return carry

        lax.fori_loop(0, n_tiles, start_tail, 0)
        lax.fori_loop(0, n_tiles, wait_tail, 0)

    @pl.when(i + 1 < nb)
    def _():
        for_pieces(i + 1, lambda src, dst, n: in_copy(1 - slot, src, dst, n).start())

    new_expert = jnp.logical_or(i == 0, be_ref[i] != be_ref[jnp.maximum(i - 1, 0)])

    @pl.when(jnp.logical_and(new_expert, i < nb))
    def _():
        winbf[...] = win_ref[0].astype(BF16)
        woutbf[...] = wout_ref[0].astype(BF16)

    @pl.when(i < nb)
    def _():
        wait_block(in_copy, i, slot)
        x = xbuf[slot].astype(BF16)
        hmid = jnp.dot(x, winbf[...], preferred_element_type=F32) + bin_ref[0]
        glu = jnp.minimum(hmid[:, :D_FF], SWIGLU_LIMIT)
        lin = jnp.clip(hmid[:, D_FF:], -SWIGLU_LIMIT, SWIGLU_LIMIT)
        act = glu * _sigmoid(SWIGLU_ALPHA * glu) * (lin + 1.0)
        y = jnp.dot(act.astype(BF16), woutbf[...], preferred_element_type=F32) + bout_ref[0]

        @pl.when(i >= 1)
        def _():
            wait_block(out_copy, i - 1, 1 - slot)

        ybuf[slot] = y
        for_pieces(i, lambda src, dst, n: out_copy(slot, src, dst, n).start())

        @pl.when(i == nb - 1)
        def _():
            wait_block(out_copy, i, slot)


def _experts(xs, tables, layer, w_in, b_in, w_out, b_out, n_tiles):
    n_blocks = tables[0].shape[0]
    depth = w_in.shape[0]

    def wmap(i, be, *_):
        return (layer, be[i], 0, 0)

    grid_spec = pltpu.PrefetchScalarGridSpec(
        num_scalar_prefetch=len(tables),
        grid=(n_blocks,),
        in_specs=[pl.BlockSpec(memory_space=pl.ANY),
                  pl.BlockSpec((None, 1, D, 2 * D_FF), wmap),
                  pl.BlockSpec((None, 1, 1, 2 * D_FF), wmap),
                  pl.BlockSpec((None, 1, D_FF, D), wmap),
                  pl.BlockSpec((None, 1, 1, D), wmap)],
        out_specs=pl.BlockSpec(memory_space=pl.ANY),
        scratch_shapes=[pltpu.VMEM((2, EXPERT_ROWS, D), F32),
                        pltpu.VMEM((2, EXPERT_ROWS, D), F32),
                        pltpu.SemaphoreType.DMA((2,)),
                        pltpu.SemaphoreType.DMA((2,)),
                        pltpu.VMEM((D, 2 * D_FF), BF16),
                        pltpu.VMEM((D_FF, D), BF16)])
    return pl.pallas_call(
        functools.partial(_expert_kernel, n_tiles=n_tiles),
        out_shape=jax.ShapeDtypeStruct(xs.shape, F32),
        grid_spec=grid_spec,
        compiler_params=_cparams(1),
        name="experts",
    )(*tables, xs, w_in, b_in.reshape(depth, N_EXPERTS, 1, 2 * D_FF), w_out, b_out.reshape(depth, N_EXPERTS, 1, D))


def _combine_kernel(ys_ref, pos_ref, gate_ref, x_ref, g_ref, lg_ref, lb_ref, o_ref):
    rows = lax.broadcasted_iota(jnp.int32, (TILE_ROWS, TILE_TOK), 0)
    perm = jnp.zeros((TILE_ROWS, TILE_TOK), F32)
    gmat = jnp.zeros((TILE_ROWS, TILE_TOK), F32)
    for k in range(TOP_K):
        hit = rows == pos_ref[0, k:k + 1, :]
        perm = perm + jnp.where(hit, 1.0, 0.0)
        gmat = gmat + jnp.where(hit, gate_ref[0, k:k + 1, :], 0.0)
    row_gate = jnp.sum(gmat, axis=1, keepdims=True)
    z = ys_ref[...] * row_gate
    hi, lo = _split2(z)
    p = perm.astype(BF16)
    dims = (((0,), (0,)), ((), ()))
    y = (lax.dot_general(p, hi, dims, preferred_element_type=F32)
         + lax.dot_general(p, lo, dims, preferred_element_type=F32))
    o_ref[...] = _layer_norm(_ALPHA * x_ref[...] + g_ref[0] * y, lg_ref[...], lb_ref[...])


def _combine(ys, pos, gates, x, mod, ln_g, ln_b, n_prompt, t_lat):
    m = x.shape[0]
    cidx = functools.partial(_cond_index, tm=TILE_TOK, n_prompt=n_prompt, t_lat=t_lat)
    row = pl.BlockSpec((TILE_TOK, D), lambda i: (i, 0))
    vec = pl.BlockSpec((1, D), lambda i: (0, 0))
    sel = pl.BlockSpec((1, TOP_K, TILE_TOK), lambda i: (i, 0, 0))
    return pl.pallas_call(
        _combine_kernel,
        out_shape=jax.ShapeDtypeStruct((m, D), F32),
        grid=(m // TILE_TOK,),
        in_specs=[pl.BlockSpec((TILE_ROWS, D), lambda i: (i, 0)), sel, sel, row,
                  pl.BlockSpec((1, 1, D), lambda i: (cidx(i), 0, 5)), vec, vec],
        out_specs=row,
        compiler_params=_cparams(1),
        name="combine",
    )(ys, pos, gates, x, mod, ln_g.reshape(1, D), ln_b.reshape(1, D))


def _routing_tables(seg_n):
    n_tiles = seg_n.shape[0]
    seg_src = (jnp.arange(n_tiles, dtype=jnp.int32)[:, None] * TILE_ROWS
               + jnp.cumsum(seg_n, axis=1) - seg_n)
    seg_cum = jnp.cumsum(seg_n, axis=0) - seg_n
    count = jnp.sum(seg_n, axis=0)
    padded = (count + EXPERT_ROWS - 1) // EXPERT_ROWS * EXPERT_ROWS
    pad_end = jnp.cumsum(padded)
    pad_start = pad_end - padded
    n_blocks = n_tiles * TILE_ROWS // EXPERT_ROWS + N_EXPERTS
    n_used = pad_end[-1] // EXPERT_ROWS
    blocks = jnp.arange(n_blocks, dtype=jnp.int32)
    owner = jnp.minimum(jnp.sum(blocks[:, None] * EXPERT_ROWS >= pad_end[None, :], axis=1), N_EXPERTS - 1)
    last_owner = jnp.minimum(jnp.sum((n_used - 1) * EXPERT_ROWS >= pad_end), N_EXPERTS - 1)
    block_expert = jnp.where(blocks < n_used, owner, last_owner).astype(jnp.int32)
    own = block_expert[:, None] == jnp.arange(N_EXPERTS, dtype=jnp.int32)[None, :]
    block_off = blocks * EXPERT_ROWS - jnp.sum(jnp.where(own, pad_start[None, :], 0), axis=1)
    block_rows = jnp.clip(jnp.sum(jnp.where(own, count[None, :], 0), axis=1) - block_off, 0, EXPERT_ROWS)
    cum_b = jnp.sum(jnp.where(own[:, None, :], seg_cum[None], 0), axis=2)
    n_b = jnp.sum(jnp.where(own[:, None, :], seg_n[None], 0), axis=2)
    t_lo = jnp.sum(cum_b + n_b <= block_off[:, None], axis=1)
    t_hi = jnp.sum(cum_b < block_off[:, None] + EXPERT_ROWS, axis=1)
    i32 = lambda a: a.astype(jnp.int32)
    return (block_expert, i32(block_off), i32(block_rows), i32(t_lo), i32(t_hi), i32(n_used).reshape(1),
            i32(seg_cum.T.reshape(-1)), i32(seg_n.T.reshape(-1)), i32(seg_src.T.reshape(-1)),
            i32(jnp.sum(seg_n, axis=1)))


def _moe(x, mod, layer, w_router, b_router, w_in, b_in, w_out, b_out, ln_g, ln_b, n_prompt, t_lat):
    xs, pos, gates, cnt = _router(x, mod, w_router, b_router, n_prompt, t_lat)
    tables = _routing_tables(cnt[:, :, 0])
    ys = _experts(xs, tables, layer, w_in, b_in, w_out, b_out, cnt.shape[0])
    return _combine(ys, pos, gates, x, mod, ln_g, ln_b, n_prompt, t_lat)


def kernel(x_prompt, x_sample, cache_k_a, cache_v_a, state_b, cache_k_c, cache_v_c, state_d_c, state_d_n, state_d_m, c, c_ctx, w_mod, b_mod, ln_g, ln_b, w_in_a, q_norm_a, k_norm_a, w_out_a, w_in_b, w_conv_b, w_gate_b, a_log_b, dt_bias_b, o_norm_b, w_out_b, w_in_c, rpb_c, w_out_c, w_in_d, w_gate_d, b_gate_d, o_norm_d, w_out_d, w_router, b_router, w_exp_in, b_exp_in, w_exp_out, b_exp_out):
    bp, t_p, _ = x_prompt.shape
    bs, t_s, _ = x_sample.shape
    past = cache_k_a.shape[2]
    depth = w_mod.shape[0]
    assert depth == 4 and bs + 1 <= N_COND
    n_p, n_s = bp * t_p, bs * t_s
    x = jnp.concatenate([x_prompt.reshape(n_p, D), x_sample.reshape(n_s, D)], axis=0)
    cond = jnp.concatenate([c_ctx.reshape(1, D), c, jnp.zeros((N_COND - 1 - bs, D), F32)], axis=0)
    res = functools.partial(_proj_res_ln, n_prompt=n_p, t_lat=t_s)
    outs = {}
    for i in range(depth):
        kind, j = i % 4, i // 4
        mod = _modulation_table(cond, w_mod, b_mod[i], i).reshape(N_COND, 1, 6 * D)
        proj = functools.partial(_modmm, x, mod, 1, 0, n_prompt=n_p, t_lat=t_s)
        if kind == 0:
            p = proj(w_in_a[j])
            q, k = _prep_a(p, q_norm_a[j], k_norm_a[j], n_p, t_s)
            hq, hk = HEADS_A * DH_A, KV_A * DH_A
            vcol = hq // hk + 1
            o_p = _attention(q, pl.BlockSpec((t_p, hq), lambda b: (b, 0)),
                             [(k, pl.BlockSpec((t_p, hk), lambda b: (b, 0)),
                               p, pl.BlockSpec((t_p, hk), lambda b: (b, vcol)))],
                             n_p, (bp,), HEADS_A, KV_A, DH_A, pl.BlockSpec((t_p, hq), lambda b: (b, 0)))
            tq = 256
            qb0, lat0 = n_p // tq, n_p // t_s
            kc = cache_k_a[:, j].reshape(bs * past, hk)
            vc = cache_v_a[:, j].reshape(bs * past, hk)
            o_s = _attention(q, pl.BlockSpec((tq, hq), lambda b, t: (qb0 + b * (t_s // tq) + t, 0)),
                             [(kc, pl.BlockSpec((past, hk), lambda b, t: (b, 0)),
                               vc, pl.BlockSpec((past, hk), lambda b, t: (b, 0))),
                              (k, pl.BlockSpec((t_s, hk), lambda b, t: (lat0 + b, 0)),
                               p, pl.BlockSpec((t_s, hk), lambda b, t: (lat0 + b, vcol)))],
                             n_s, (bs, t_s // tq), HEADS_A, KV_A, DH_A,
                             pl.BlockSpec((tq, hq), lambda b, t: (b * (t_s // tq) + t, 0)))
            outs["k_a"] = k[:n_p].reshape(bp, 1, t_p, KV_A, DH_A)
            outs["v_a"] = p[:n_p, hq + hk:].reshape(bp, 1, t_p, KV_A, DH_A)
            x = res("plain", (o_p, o_s), w_out_a[j], x, mod, 2, ln_g[i, 0], ln_b[i, 0])
        elif kind == 1:
            p = proj(w_in_b[j])
            raw = proj(w_gate_b[j])
            zeros8 = jnp.zeros((2, HEADS_B), F32)
            a32 = jnp.concatenate([a_log_b[j], zeros8], axis=1).reshape(1, 32)
            dt32 = jnp.concatenate([dt_bias_b[j], zeros8], axis=1).reshape(1, 32)
            gcol, grow = _gate_layouts(_gates("gdn", raw, a32, dt32))
            qp, kp, vp = _prep_b(p, w_conv_b[j], 0, n_p, t_p)
            qs, ks, vs = _prep_b(p, w_conv_b[j], n_p, n_s, t_s)
            o_pf, o_pb, s_p = _gdn(qp, kp, vp, gcol, grow, None, bp, t_p, 0)
            o_sf, o_sb, _ = _gdn(qs, ks, vs, gcol, grow, state_b[:, j], bs, t_s, n_p // CHUNK)
            outs["s_b"] = s_p.reshape(bp, 1, 2, HEADS_B, DK_B, DK_B)
            x = res("silu", (o_pf, o_pb, o_sf, o_sb, p, o_norm_b[j].reshape(1, DK_B)), w_out_b[j], x, mod, 2, ln_g[i, 0], ln_b[i, 0],
                    z_colblock=3)
        elif kind == 2:
            hd = HEADS_C * DH_C
            pq, pk, pv = (proj(w_in_c[j][:, s * hd:(s + 1) * hd]) for s in range(3))
            blk = pl.BlockSpec((t_p, hd), lambda b: (b, 0))
            o_p = _attention(pq, blk, [(pk, blk, pv, blk)], n_p, (bp,), HEADS_C, HEADS_C, DH_C, blk)
            o_s = _nbr_attention(pq, pk, pv, cache_k_c[:, j].reshape(bs * past, hd),
                                 cache_v_c[:, j].reshape(bs * past, hd), rpb_c[j], n_p, bs, t_s, past)
            outs["k_c"] = pk[:n_p].reshape(bp, 1, t_p, HEADS_C, DH_C)
            outs["v_c"] = pv[:n_p].reshape(bp, 1, t_p, HEADS_C, DH_C)
            x = res("plain", (o_p, o_s), w_out_c[j], x, mod, 2, ln_g[i, 0], ln_b[i, 0])
        else:
            p = proj(w_in_d[j])
            raw = proj(w_gate_d[j])
            gcol, grow = _gate_layouts(_gates("mlstm", raw, b_gate_d[j].reshape(1, 32), jnp.zeros((1, 32), F32)))
            h_pf, h_pb, c_p, n_fin, m_fin = _mlstm(p, gcol, grow, None, bp, t_p, 0)
            m0 = jnp.broadcast_to(state_d_m[:, j][..., None], (bs, 2, HEADS_D, 128))
            h_sf, h_sb, _, _, _ = _mlstm(p, gcol, grow, (state_d_c[:, j], state_d_n[:, j], m0), bs, t_s, n_p)
            outs["c_d"] = c_p.reshape(bp, 1, 2, HEADS_D, DK_D, DV_D)
            outs["n_d"] = n_fin.reshape(bp, 1, 2, HEADS_D, DK_D)
            outs["m_d"] = m_fin[..., 0].reshape(bp, 1, 2, HEADS_D)
            x = res("sigmoid", (h_pf, h_pb, h_sf, h_sb, p, o_norm_d[j].reshape(1, DV_D)), w_out_d[j], x, mod, 2, ln_g[i, 0], ln_b[i, 0],
                    z_colblock=2)
        x = _moe(x, mod, i, w_router[i], b_router[i], w_exp_in, b_exp_in, w_exp_out, b_exp_out,
                 ln_g[i, 1], ln_b[i, 1], n_p, t_s)
    return (x[:n_p].reshape(bp, t_p, D), x[n_p:].reshape(bs, t_s, D), outs["k_a"], outs["v_a"], outs["s_b"],
            outs["k_c"], outs["v_c"], outs["c_d"], outs["n_d"], outs["m_d"])
```

```python
import functools

import jax
import jax.numpy as jnp
import numpy as np
from jax import lax
from jax.experimental import pallas as pl
from jax.experimental.pallas import tpu as pltpu

F32 = jnp.float32
BF16 = jnp.bfloat16

D = 1024
GRID_W = 64
HEADS_A, KV_A, DH_A = 16, 4, 64
ROPE_THETA = 10000.0
HEADS_B, DK_B = 8, 128
CONV_W = 3
CHUNK = 64
HEADS_C, DH_C = 16, 64
WIN_R, WIN_C = 8, 16
HEADS_D, DK_D, DV_D = 8, 64, 128
N_EXPERTS, TOP_K, D_FF = 32, 4, 1024
SWIGLU_LIMIT, SWIGLU_ALPHA = 7.0, 1.702
NEG_INF = -1e30
N_COND = 8
EXPERT_ROWS = 512
TILE_TOK = 256
SEG_ALIGN = 8
TILE_ROWS = (TILE_TOK * TOP_K + N_EXPERTS * (SEG_ALIGN - 1) + 7) // 8 * 8
VMEM_LIMIT = 56 * 1024 * 1024
_ALPHA = float((2 * 4) ** 0.25)


def _cparams(n_axes):
    return pltpu.CompilerParams(dimension_semantics=("arbitrary",) * n_axes, vmem_limit_bytes=VMEM_LIMIT)


def _dot(a, b):
    return jnp.dot(a.astype(BF16), b.astype(BF16), preferred_element_type=F32)


def _dot_nt(a, b):
    return lax.dot_general(a.astype(BF16), b.astype(BF16), (((1,), (1,)), ((), ())), preferred_element_type=F32)


def _dot_tn(a, b):
    return lax.dot_general(a.astype(BF16), b.astype(BF16), (((0,), (0,)), ((), ())), preferred_element_type=F32)


def _split2(x):
    hi = x.astype(BF16)
    lo = (x - hi.astype(F32)).astype(BF16)
    return hi, lo


def _split3(x):
    hi = x.astype(BF16)
    r = x - hi.astype(F32)
    mid = r.astype(BF16)
    lo = (r - mid.astype(F32)).astype(BF16)
    return hi, mid, lo


def _dot_hi(a, b):
    ah, al = _split2(a)
    bh, bl = _split2(b)
    d = functools.partial(jnp.dot, preferred_element_type=F32)
    return d(ah, bh) + (d(ah, bl) + d(al, bh))


def _mask_dot(mask_bf16, x):
    d = functools.partial(jnp.dot, preferred_element_type=F32)
    hi, mid, lo = _split3(x)
    return d(mask_bf16, hi) + (d(mask_bf16, mid) + d(mask_bf16, lo))


def _dot_mask(x, mask_bf16):
    d = functools.partial(jnp.dot, preferred_element_type=F32)
    hi, mid, lo = _split3(x)
    return d(hi, mask_bf16) + (d(mid, mask_bf16) + d(lo, mask_bf16))


def _layer_norm(x, g, b):
    mu = jnp.mean(x, axis=-1, keepdims=True)
    xc = x - mu
    var = jnp.mean(xc * xc, axis=-1, keepdims=True)
    return xc * lax.rsqrt(var + 1e-5) * g + b


def _sigmoid(x):
    return 1.0 / (1.0 + jnp.exp(-x))


def _cond_index(i, tm, n_prompt, t_lat):
    npt = n_prompt // tm
    return jnp.where(i < npt, 0, 1 + (i - npt) // (t_lat // tm))


def _mod_kernel(c_ref, w_ref, b_ref, o_ref):
    c = c_ref[...]
    o_ref[...] = _dot(c * _sigmoid(c), w_ref[...]) + b_ref[...]


def _modulation_table(cond, w, b, layer):
    n = w.shape[2]
    tn = 1024
    return pl.pallas_call(
        _mod_kernel,
        out_shape=jax.ShapeDtypeStruct((N_COND, n), F32),
        grid=(n // tn,),
        in_specs=[pl.BlockSpec((N_COND, D), lambda j: (0, 0)),
                  pl.BlockSpec((None, D, tn), lambda j: (layer, 0, j)),
                  pl.BlockSpec((1, tn), lambda j: (0, j))],
        out_specs=pl.BlockSpec((N_COND, tn), lambda j: (0, j)),
        compiler_params=_cparams(1),
        name="mod_table",
    )(cond, w, b.reshape(1, n))


def _modmm_kernel(x_ref, sc_ref, sh_ref, w_ref, o_ref, wbf_ref):
    @pl.when(pl.program_id(1) == 0)
    def _():
        wbf_ref[...] = w_ref[...].astype(BF16)

    h = x_ref[...] * (1.0 + sc_ref[0]) + sh_ref[0]
    o_ref[...] = jnp.dot(h.astype(BF16), wbf_ref[...], preferred_element_type=F32)


def _col_tile(n):
    for t in (1024, 512, 256, 128):
        if n % t == 0:
            return t
    return n


def _modmm(x, mod, which_sc, which_sh, w, n_prompt, t_lat, tm=512):
    m = x.shape[0]
    n = w.shape[1]
    tn = _col_tile(n)
    cidx = functools.partial(_cond_index, tm=tm, n_prompt=n_prompt, t_lat=t_lat)
    return pl.pallas_call(
        _modmm_kernel,
        out_shape=jax.ShapeDtypeStruct((m, n), F32),
        grid=(n // tn, m // tm),
        in_specs=[pl.BlockSpec((tm, D), lambda j, i: (i, 0)),
                  pl.BlockSpec((1, 1, D), lambda j, i: (cidx(i), 0, which_sc)),
                  pl.BlockSpec((1, 1, D), lambda j, i: (cidx(i), 0, which_sh)),
                  pl.BlockSpec((D, tn), lambda j, i: (0, j))],
        out_specs=pl.BlockSpec((tm, tn), lambda j, i: (i, j)),
        scratch_shapes=[pltpu.VMEM((D, tn), BF16)],
        compiler_params=_cparams(2),
        name="mod_proj",
    )(x, mod, mod, w)


def _proj_kernel(*refs, kind, prompt_tiles):
    if kind == "plain":
        ap_ref, as_ref, w_ref, x_ref, g_ref, lg_ref, lb_ref, o_ref, wbf_ref = refs
    else:
        (pf_ref, pb_ref, sf_ref, sb_ref, z_ref, on_ref, w_ref, x_ref, g_ref, lg_ref, lb_ref,
         o_ref, wbf_ref, abuf_ref) = refs
    is_prompt = pl.program_id(0) < prompt_tiles

    @pl.when(pl.program_id(0) == 0)
    def _():
        wbf_ref[...] = w_ref[...].astype(BF16)

    if kind == "plain":
        a = jnp.where(is_prompt, ap_ref[...], as_ref[...]).astype(BF16)
    else:
        s = jnp.where(is_prompt, pf_ref[...] + pb_ref[...], sf_ref[...] + sb_ref[...])
        on = on_ref[...]
        hd = on.shape[-1]
        for h in range(D // hd):
            sl = slice(h * hd, (h + 1) * hd)
            sh = s[:, sl]
            y = sh * lax.rsqrt(jnp.mean(sh * sh, axis=-1, keepdims=True) + 1e-6) * on
            z = z_ref[:, sl]
            gate = _sigmoid(z)
            if kind == "silu":
                gate = z * gate
            abuf_ref[:, sl] = (y * gate).astype(BF16)
        a = abuf_ref[...]
    y = jnp.dot(a, wbf_ref[...], preferred_element_type=F32)
    o_ref[...] = _layer_norm(_ALPHA * x_ref[...] + g_ref[0] * y, lg_ref[...], lb_ref[...])


def _proj_res_ln(kind, a_args, w, x, mod, which_g, ln_g, ln_b, n_prompt, t_lat, z_colblock=0, tm=256):
    m = x.shape[0]
    cidx = functools.partial(_cond_index, tm=tm, n_prompt=n_prompt, t_lat=t_lat)
    row = pl.BlockSpec((tm, D), lambda i: (i, 0))
    vec = pl.BlockSpec((1, D), lambda i: (0, 0))
    npt = n_prompt // tm
    prompt_rows = pl.BlockSpec((tm, D), lambda i: (jnp.minimum(i, npt - 1), 0))
    latent_rows = pl.BlockSpec((tm, D), lambda i: (jnp.maximum(i - npt, 0), 0))
    if kind == "plain":
        a_specs = [prompt_rows, latent_rows]
        scratch = [pltpu.VMEM((D, D), BF16)]
    else:
        on = a_args[5]
        a_specs = [prompt_rows, prompt_rows, latent_rows, latent_rows,
                   pl.BlockSpec((tm, D), lambda i: (i, z_colblock)),
                   pl.BlockSpec((1, on.shape[-1]), lambda i: (0, 0))]
        scratch = [pltpu.VMEM((D, D), BF16), pltpu.VMEM((tm, D), BF16)]
    return pl.pallas_call(
        functools.partial(_proj_kernel, kind=kind, prompt_tiles=npt),
        out_shape=jax.ShapeDtypeStruct((m, D), F32),
        grid=(m // tm,),
        in_specs=a_specs + [pl.BlockSpec((D, D), lambda i: (0, 0)), row,
                            pl.BlockSpec((1, 1, D), lambda i: (cidx(i), 0, which_g)), vec, vec],
        out_specs=row,
        scratch_shapes=scratch,
        compiler_params=_cparams(1),
        name="out_proj_" + kind,
    )(*a_args, w, x, mod, ln_g.reshape(1, D), ln_b.reshape(1, D))


def _prep_a_kernel(p_ref, qn_ref, kn_ref, cos_ref, sin_ref, q_ref, k_ref):
    cos = cos_ref[...]
    sin = sin_ref[...]
    width = 2 * DH_A
    lane = lax.broadcasted_iota(jnp.int32, cos.shape, 1)
    first_half = lane % DH_A < DH_A // 2
    r = lax.broadcasted_iota(jnp.int32, (width, width), 0)
    c = lax.broadcasted_iota(jnp.int32, (width, width), 1)
    same_head = jnp.where(r // DH_A == c // DH_A, 1.0, 0.0).astype(BF16)

    def pair(x, g):
        ms = _dot_mask(x * x, same_head) * (1.0 / DH_A)
        y = x * lax.rsqrt(ms + 1e-6) * g
        swapped = jnp.where(first_half, pltpu.roll(y, width - DH_A // 2, axis=1), pltpu.roll(y, DH_A // 2, axis=1))
        return y * cos + swapped * sin

    for j in range(HEADS_A // 2):
        sl = slice(j * width, (j + 1) * width)
        q_ref[:, sl] = pair(p_ref[:, sl], qn_ref[...])
    for j in range(KV_A // 2):
        sl = slice(j * width, (j + 1) * width)
        k_ref[:, sl] = pair(p_ref[:, HEADS_A * DH_A + j * width: HEADS_A * DH_A + (j + 1) * width], kn_ref[...])


def _rope_tables(t_lat, tm):
    quarter = DH_A // 4
    t = np.arange(t_lat)
    inv_freq = ROPE_THETA ** (-np.arange(quarter, dtype=np.float32) / quarter)
    ang = np.concatenate([(t // GRID_W)[:, None] * inv_freq, (t % GRID_W)[:, None] * inv_freq], axis=-1)
    ang = jnp.asarray(ang, F32)
    cos, sin = jnp.cos(ang), jnp.sin(ang)
    cos64 = jnp.concatenate([cos, cos], axis=-1)
    sin64 = jnp.concatenate([-sin, sin], axis=-1)
    cos64 = jnp.concatenate([jnp.ones((tm, DH_A), F32), cos64], axis=0)
    sin64 = jnp.concatenate([jnp.zeros((tm, DH_A), F32), sin64], axis=0)
    return jnp.tile(cos64, (1, 2)), jnp.tile(sin64, (1, 2))


def _prep_a(p, q_norm, k_norm, n_prompt, t_lat, tm=256):
    m = p.shape[0]
    cos64, sin64 = _rope_tables(t_lat, tm)
    npt = n_prompt // tm

    def tab(i):
        return (jnp.where(i < npt, 0, 1 + (i - npt) % (t_lat // tm)), 0)

    return pl.pallas_call(
        _prep_a_kernel,
        out_shape=(jax.ShapeDtypeStruct((m, HEADS_A * DH_A), F32), jax.ShapeDtypeStruct((m, KV_A * DH_A), F32)),
        grid=(m // tm,),
        in_specs=[pl.BlockSpec((tm, p.shape[1]), lambda i: (i, 0)),
                  pl.BlockSpec((1, 2 * DH_A), lambda i: (0, 0)),
                  pl.BlockSpec((1, 2 * DH_A), lambda i: (0, 0)),
                  pl.BlockSpec((tm, 2 * DH_A), tab),
                  pl.BlockSpec((tm, 2 * DH_A), tab)],
        out_specs=(pl.BlockSpec((tm, HEADS_A * DH_A), lambda i: (i, 0)),
                   pl.BlockSpec((tm, KV_A * DH_A), lambda i: (i, 0))),
        compiler_params=_cparams(1),
        name="prep_a",
    )(p, jnp.tile(q_norm.reshape(1, DH_A), (1, 2)), jnp.tile(k_norm.reshape(1, DH_A), (1, 2)), cos64, sin64)


def _softmax_pv(scores, values):
    m = scores[0].max(axis=-1, keepdims=True)
    for s in scores[1:]:
        m = jnp.maximum(m, s.max(axis=-1, keepdims=True))
    ps = [jnp.exp(s - m) for s in scores]
    den = ps[0].sum(axis=-1, keepdims=True)
    for p in ps[1:]:
        den = den + p.sum(axis=-1, keepdims=True)
    o = _dot(ps[0], values[0])
    for p, v in zip(ps[1:], values[1:]):
        o = o + _dot(p, v)
    return o * (1.0 / den)


def _pipelined_heads(n_heads, score_fn, finish_fn, batch=4):
    pending = []
    for h0 in range(0, n_heads, batch):
        current = [(h, score_fn(h)) for h in range(h0, min(h0 + batch, n_heads))]
        for h, sc in pending:
            finish_fn(h, sc)
        pending = current
    for h, sc in pending:
        finish_fn(h, sc)


def _attn_kernel(*refs, heads, kv_heads, dh, nseg):
    q_ref = refs[0]
    kv = refs[1:1 + 2 * nseg]
    o_ref = refs[1 + 2 * nseg]
    scale = dh ** -0.5
    group = heads // kv_heads

    def scores(h):
        ksl = slice(h // group * dh, (h // group + 1) * dh)
        q = q_ref[:, h * dh:(h + 1) * dh] * scale
        return [_dot_nt(q, kv[2 * s][:, ksl]) for s in range(nseg)]

    def finish(h, sc):
        ksl = slice(h // group * dh, (h // group + 1) * dh)
        o_ref[:, h * dh:(h + 1) * dh] = _softmax_pv(sc, [kv[2 * s + 1][:, ksl] for s in range(nseg)])

    _pipelined_heads(heads, scores, finish)


def _attention(q_arr, q_spec, segs, out_rows, grid, heads, kv_heads, dh, out_spec):
    in_specs = [q_spec]
    args = [q_arr]
    for (k_arr, k_spec, v_arr, v_spec) in segs:
        in_specs += [k_spec, v_spec]
        args += [k_arr, v_arr]
    return pl.pallas_call(
        functools.partial(_attn_kernel, heads=heads, kv_heads=kv_heads, dh=dh, nseg=len(segs)),
        out_shape=jax.ShapeDtypeStruct((out_rows, heads * dh), F32),
        grid=grid,
        in_specs=in_specs,
        out_specs=out_spec,
        compiler_params=_cparams(len(grid)),
        name="attention",
    )(*args)


def _window_start(r, n_rows):
    wr = min(WIN_R, n_rows)
    return jnp.minimum(jnp.maximum(r - wr // 2, 0), n_rows - wr)


def _nbr_kernel(q_ref, kl_ref, vl_ref, kc_ref, vc_ref, bias_ref, o_ref, *, n_rows):
    r = pl.program_id(1)
    wr = min(WIN_R, n_rows)
    start = pl.multiple_of(_window_start(r, n_rows) * GRID_W, GRID_W)
    scale = DH_C ** -0.5
    window = pl.ds(start, wr * GRID_W)

    def scores(h):
        sl = slice(h * DH_C, (h + 1) * DH_C)
        q = q_ref[:, sl] * scale
        return [_dot_nt(q, kl_ref[window, sl]) + bias_ref[0, h], _dot_nt(q, kc_ref[:, sl])]

    def finish(h, sc):
        sl = slice(h * DH_C, (h + 1) * DH_C)
        o_ref[:, sl] = _softmax_pv(sc, [vl_ref[window, sl], vc_ref[:, sl]])

    _pipelined_heads(HEADS_C, scores, finish)


def _nbr_bias(rpb, n_rows):
    wr = min(WIN_R, n_rows)
    cols = np.arange(GRID_W)
    col_start = np.clip(cols - WIN_C // 2, 0, GRID_W - WIN_C)
    col_ok = (cols[None, :] >= col_start[:, None]) & (cols[None, :] < col_start[:, None] + WIN_C)
    dc = np.clip(cols[None, :] - cols[:, None], -(WIN_C - 1), WIN_C - 1) + (WIN_C - 1)
    n_off = 2 * WIN_R - wr
    pick = np.zeros((2 * WIN_C - 1, GRID_W * GRID_W), np.float32)
    pick[dc.reshape(-1), np.arange(GRID_W * GRID_W)] = 1.0
    tab = jnp.einsum("hrd,dx->hrx", rpb.astype(F32), jnp.asarray(pick), precision=lax.Precision.HIGHEST)
    tab = jnp.where(col_ok[None, None], tab.reshape(HEADS_C, 2 * WIN_R - 1, GRID_W, GRID_W), NEG_INF)
    b = jnp.stack([tab[:, off:off + wr] for off in range(n_off)], axis=0)
    return jnp.transpose(b, (0, 1, 3, 2, 4)).reshape(n_off, HEADS_C, GRID_W, wr * GRID_W)


def _nbr_attention(q, k, v, k_ctx, v_ctx, rpb, n_prompt, n_lat_b, t_lat, past):
    n_rows = t_lat // GRID_W
    wr = min(WIN_R, n_rows)
    bias = _nbr_bias(rpb, n_rows)
    row0 = n_prompt // GRID_W
    lat0 = n_prompt // t_lat
    hd = HEADS_C * DH_C

    def off(b, r):
        return (_window_start(r, n_rows) - r + (WIN_R - 1), 0, 0, 0)

    return pl.pallas_call(
        functools.partial(_nbr_kernel, n_rows=n_rows),
        out_shape=jax.ShapeDtypeStruct((n_lat_b * t_lat, hd), F32),
        grid=(n_lat_b, n_rows),
        in_specs=[pl.BlockSpec((GRID_W, hd), lambda b, r: (row0 + b * n_rows + r, 0)),
                  pl.BlockSpec((t_lat, hd), lambda b, r: (lat0 + b, 0)),
                  pl.BlockSpec((t_lat, hd), lambda b, r: (lat0 + b, 0)),
                  pl.BlockSpec((past, hd), lambda b, r: (b, 0)),
                  pl.BlockSpec((past, hd), lambda b, r: (b, 0)),
                  pl.BlockSpec((1, HEADS_C, GRID_W, wr * GRID_W), off)],
        out_specs=pl.BlockSpec((GRID_W, hd), lambda b, r: (b * n_rows + r, 0)),
        compiler_params=_cparams(2),
        name="nbr_attention",
    )(q, k, v, k_ctx, v_ctx, bias)


def _chunk_masks(d):
    r = lax.broadcasted_iota(jnp.int32, (CHUNK, CHUNK), 0)
    c = lax.broadcasted_iota(jnp.int32, (CHUNK, CHUNK), 1)
    diff = r - c if d == 0 else c - r
    incl = diff >= 0
    strict = diff > 0
    incl_t = diff <= 0
    return incl, strict, incl_t


def _gate_kernel(raw_ref, p0_ref, p1_ref, o_ref, *, kind):
    raw = raw_ref[...]
    lane = lax.broadcasted_iota(jnp.int32, raw.shape, 1)
    first = (lane % 16) < 8
    if kind == "gdn":
        g = -jnp.exp(p0_ref[...]) * jax.nn.softplus(raw + p1_ref[...])
        o_ref[...] = jnp.where(first, g, _sigmoid(raw))
    else:
        x = raw + p0_ref[...]
        o_ref[...] = jnp.where(first, x, -jax.nn.softplus(-x))


def _gates(kind, raw, p0, p1, tm=512):
    m, n = raw.shape
    return pl.pallas_call(
        functools.partial(_gate_kernel, kind=kind),
        out_shape=jax.ShapeDtypeStruct((m, n), F32),
        grid=(m // tm,),
        in_specs=[pl.BlockSpec((tm, n), lambda i: (i, 0)),
                  pl.BlockSpec((1, n), lambda i: (0, 0)),
                  pl.BlockSpec((1, n), lambda i: (0, 0))],
        out_specs=pl.BlockSpec((tm, n), lambda i: (i, 0)),
        compiler_params=_cparams(1),
        name="gates_" + kind,
    )(raw, p0, p1)


def _gate_layouts(g):
    m = g.shape[0]
    col = g.reshape(m // CHUNK, CHUNK, 2, 16).transpose(2, 0, 1, 3)
    return col, col.transpose(0, 1, 3, 2)


def _prep_b_kernel(p_ref, prev_ref, next_ref, w_ref, q_ref, k_ref, v_ref, *, tm, tiles_per_seq):
    i = pl.program_id(0)
    pos = i % tiles_per_seq
    rows = lax.broadcasted_iota(jnp.int32, (tm, D), 0)
    outs = (q_ref, k_ref, v_ref)
    for seg in range(3):
        sl = slice(seg * D, (seg + 1) * D)
        x = p_ref[:, sl]
        prev_row = jnp.where(pos == 0, 0.0, prev_ref[7:8, sl])
        next_row = jnp.where(pos == tiles_per_seq - 1, 0.0, next_ref[0:1, sl])
        x_prev = jnp.where(rows == 0, prev_row, pltpu.roll(x, 1, axis=0))
        x_next = jnp.where(rows == tm - 1, next_row, pltpu.roll(x, tm - 1, axis=0))
        y = x_prev * w_ref[0:1, sl] + x * w_ref[1:2, sl] + x_next * w_ref[2:3, sl]
        y = y * _sigmoid(y)
        if seg < 2:
            for h in range(HEADS_B):
                hs = slice(h * DK_B, (h + 1) * DK_B)
                yh = y[:, hs]
                outs[seg][:, hs] = yh * lax.rsqrt(jnp.sum(yh * yh, axis=-1, keepdims=True) + 1e-6)
        else:
            v_ref[...] = y


def _prep_b(p, w_conv, row0, n_rows, t_seq, tm=256):
    total8 = p.shape[0] // 8
    r0 = row0 // tm
    tiles_per_seq = t_seq // tm
    out = jax.ShapeDtypeStruct((n_rows, D), F32)
    ospec = pl.BlockSpec((tm, D), lambda i: (i, 0))
    return pl.pallas_call(
        functools.partial(_prep_b_kernel, tm=tm, tiles_per_seq=tiles_per_seq),
        out_shape=(out, out, out),
        grid=(n_rows // tm,),
        in_specs=[pl.BlockSpec((tm, p.shape[1]), lambda i: (r0 + i, 0)),
                  pl.BlockSpec((8, p.shape[1]), lambda i: (jnp.maximum((r0 + i) * (tm // 8) - 1, 0), 0)),
                  pl.BlockSpec((8, p.shape[1]), lambda i: (jnp.minimum((r0 + i + 1) * (tm // 8), total8 - 1), 0)),
                  pl.BlockSpec((CONV_W, 3 * D), lambda i: (0, 0))],
        out_specs=(ospec, ospec, ospec),
        compiler_params=_cparams(1),
        name="prep_b",
    )(p, p, p, w_conv)


def _unit_tri_inverses(mats):
    r = lax.broadcasted_iota(jnp.int32, mats[0].shape, 0)
    c = lax.broadcasted_iota(jnp.int32, mats[0].shape, 1)
    eye = jnp.where(r == c, 1.0, 0.0)
    ps = [-a for a in mats]
    ts = [eye + p for p in ps]
    for _ in range(int(np.log2(CHUNK)) - 1):
        ps = [_dot_hi(p, p) for p in ps]
        ts = [t + _dot_hi(t, p) for t, p in zip(ts, ps)]
    return ts


def _gdn_kernel(*refs, n_chunks, zero_init):
    ins, rest = refs[:10], refs[10:]
    if not zero_init:
        s0_ref, rest = rest[0], rest[1:]
    o_refs, (sfin_ref, s_ref) = rest[:2], rest[2:]
    c = pl.program_id(1)

    @pl.when(c == 0)
    def _():
        if zero_init:
            s_ref[...] = jnp.zeros(s_ref.shape, F32)
        else:
            s_ref[...] = s0_ref[0]

    scale = DK_B ** -0.5
    qh, kh, vh, gc_c, b_c, gl, states, decay, strict = [], [], [], [], [], [], [], [], []
    for d in range(2):
        q_ref, k_ref, v_ref, gc_ref, gr_ref = ins[5 * d:5 * d + 5]
        incl, strict_d, incl_t = _chunk_masks(d)
        gcol = gc_ref[0, 0]
        grow = gr_ref[0, 0]
        gcs = _mask_dot(jnp.where(incl, 1.0, 0.0).astype(BF16), gcol)
        grs = _dot_mask(grow, jnp.where(incl_t, 1.0, 0.0).astype(BF16))
        last = CHUNK - 1 if d == 0 else 0
        q_all, k_all, v_all = q_ref[...], k_ref[...], v_ref[...]
        for h in range(HEADS_B):
            hs = slice(h * DK_B, (h + 1) * DK_B)
            qh.append(q_all[:, hs] * scale)
            kh.append(k_all[:, hs])
            vh.append(v_all[:, hs])
            gc_c.append(gcs[:, h:h + 1])
            b_c.append(gcol[:, HEADS_B + h:HEADS_B + h + 1])
            gl.append(gcs[last:last + 1, h:h + 1])
            states.append(s_ref[d, h])
            decay.append(jnp.exp(jnp.where(incl, gcs[:, h:h + 1] - grs[h:h + 1, :], NEG_INF)))
            strict.append(strict_d)
    chains = range(2 * HEADS_B)
    egc = [jnp.exp(g) for g in gc_c]
    kb = [kh[i] * b_c[i] for i in chains]
    a_mat = [jnp.where(strict[i], _dot_nt(kb[i], kh[i]) * decay[i], 0.0) for i in chains]
    attn = [_dot_nt(qh[i], kh[i]) * decay[i] for i in chains]
    o_state = [_dot(qh[i] * egc[i], states[i]) for i in chains]
    t_inv = _unit_tri_inverses(a_mat)
    u = [_dot_hi(t_inv[i], vh[i] * b_c[i]) for i in chains]
    w = [_dot_hi(t_inv[i], kb[i] * egc[i]) for i in chains]
    v_new = [u[i] - _dot(w[i], states[i]) for i in chains]
    outs = [o_state[i] + _dot(attn[i], v_new[i]) for i in chains]
    for d in range(2):
        o_refs[d][...] = jnp.concatenate(outs[d * HEADS_B:(d + 1) * HEADS_B], axis=1)
    for i in chains:
        s_ref[i // HEADS_B, i % HEADS_B] = (states[i] * jnp.exp(gl[i])
                                            + _dot_tn(kh[i] * jnp.exp(gl[i] - gc_c[i]), v_new[i]))

    @pl.when(c == n_chunks - 1)
    def _():
        sfin_ref[0] = s_ref[...]


def _chunk_index(d, c, n_chunks):
    return c + d * (n_chunks - 1 - 2 * c)


def _gdn(q, k, v, gcol, grow, s0, n_seq, t_seq, chunk0):
    n_chunks = t_seq // CHUNK
    in_specs, args, out_specs = [], [], []
    for d in range(2):
        def rows(b, c, d=d):
            return (b * n_chunks + _chunk_index(d, c, n_chunks), 0)

        def gidx(b, c, d=d):
            return (d, chunk0 + b * n_chunks + _chunk_index(d, c, n_chunks), 0, 0)

        blk = pl.BlockSpec((CHUNK, D), rows)
        in_specs += [blk, blk, blk, pl.BlockSpec((1, 1, CHUNK, 16), gidx), pl.BlockSpec((1, 1, 16, CHUNK), gidx)]
        args += [q, k, v, gcol, grow]
        out_specs.append(blk)
    state_spec = pl.BlockSpec((1, 2, HEADS_B, DK_B, DK_B), lambda b, c: (b, 0, 0, 0, 0))
    if s0 is not None:
        in_specs.append(state_spec)
        args.append(s0)
    m = n_seq * t_seq
    return pl.pallas_call(
        functools.partial(_gdn_kernel, n_chunks=n_chunks, zero_init=s0 is None),
        out_shape=(jax.ShapeDtypeStruct((m, D), F32), jax.ShapeDtypeStruct((m, D), F32),
                   jax.ShapeDtypeStruct((n_seq, 2, HEADS_B, DK_B, DK_B), F32)),
        grid=(n_seq, n_chunks),
        in_specs=in_specs,
        out_specs=tuple(out_specs) + (state_spec,),
        scratch_shapes=[pltpu.VMEM((2, HEADS_B, DK_B, DK_B), F32)],
        compiler_params=_cparams(2),
        name="gdn",
    )(*args)


def _mlstm_kernel(*refs, n_chunks, zero_init):
    ins, rest = refs[:8], refs[8:]
    if not zero_init:
        (c0_ref, n0_ref, m0_ref), rest = rest[:3], rest[3:]
    o_refs, (cfin_ref, nfin_ref, mfin_ref, c_ref, n_ref, m_ref) = rest[:2], rest[2:]
    c = pl.program_id(1)

    @pl.when(c == 0)
    def _():
        if zero_init:
            c_ref[...] = jnp.zeros(c_ref.shape, F32)
            n_ref[...] = jnp.zeros(n_ref.shape, F32)
            m_ref[...] = jnp.zeros(m_ref.shape, F32)
        else:
            c_ref[...] = c0_ref[0]
            n_ref[...] = n0_ref[0]
            m_ref[...] = m0_ref[0]

    scale = DK_D ** -0.5
    qh, kh, vh, b_c, ig_c, m_old, n_old, c_old, d_intra, b_last = [], [], [], [], [], [], [], [], [], []
    for d in range(2):
        qk_ref, v_ref, gc_ref, gr_ref = ins[4 * d:4 * d + 4]
        incl, _, incl_t = _chunk_masks(d)
        gcol = gc_ref[0, 0]
        grow = gr_ref[0, 0]
        bcs = _mask_dot(jnp.where(incl, 1.0, 0.0).astype(BF16), gcol)
        brs = _dot_mask(grow, jnp.where(incl_t, 1.0, 0.0).astype(BF16))
        last = CHUNK - 1 if d == 0 else 0
        qk_all, v_all = qk_ref[...], v_ref[...]
        m_all, n_all = m_ref[d], n_ref[d]
        for h in range(HEADS_D):
            qh.append(qk_all[:, h * DK_D:(h + 1) * DK_D] * scale)
            kh.append(qk_all[:, HEADS_D * DK_D + h * DK_D: HEADS_D * DK_D + (h + 1) * DK_D])
            vh.append(v_all[:, h * DV_D:(h + 1) * DV_D])
            b_c.append(bcs[:, HEADS_D + h:HEADS_D + h + 1])
            ig_c.append(gcol[:, h:h + 1])
            m_old.append(m_all[h:h + 1, 0:1])
            n_old.append(n_all[h:h + 1, :])
            c_old.append(c_ref[d, h])
            d_intra.append(jnp.where(incl, bcs[:, HEADS_D + h:HEADS_D + h + 1]
                                     - brs[HEADS_D + h:HEADS_D + h + 1, :] + grow[h:h + 1, :], NEG_INF))
            b_last.append(bcs[last:last + 1, HEADS_D + h:HEADS_D + h + 1])
    heads = range(2 * HEADS_D)
    a_inter = [b_c[h] + m_old[h] for h in heads]
    m_t = [jnp.maximum(a_inter[h], jnp.max(d_intra[h], axis=-1, keepdims=True)) for h in heads]
    w_inter = [jnp.exp(a_inter[h] - m_t[h]) for h in heads]
    s = [_dot_nt(qh[h], kh[h]) * jnp.exp(d_intra[h] - m_t[h]) for h in heads]
    q_state = [_dot(qh[h], c_old[h]) for h in heads]
    d_state = [b_last[h] - b_c[h] + ig_c[h] for h in heads]
    m_new = [jnp.maximum(b_last[h] + m_old[h], jnp.max(d_state[h], axis=0, keepdims=True)) for h in heads]
    w_old = [jnp.exp(b_last[h] + m_old[h] - m_new[h]) for h in heads]
    kw = [kh[h] * jnp.exp(d_state[h] - m_new[h]) for h in heads]
    kv = [_dot_tn(kw[h], vh[h]) for h in heads]
    outs = []
    for h in heads:
        num = w_inter[h] * q_state[h] + _dot(s[h], vh[h])
        qn = jnp.sum(qh[h] * n_old[h], axis=-1, keepdims=True)
        den = w_inter[h] * qn + jnp.sum(s[h], axis=-1, keepdims=True)
        outs.append(num / jnp.maximum(jnp.abs(den), jnp.exp(-m_t[h])))
    for h in heads:
        c_ref[h // HEADS_D, h % HEADS_D] = w_old[h] * c_old[h] + kv[h]
    for d in range(2):
        hd = range(d * HEADS_D, (d + 1) * HEADS_D)
        o_refs[d][...] = jnp.concatenate([outs[h] for h in hd], axis=1)
        n_ref[d] = jnp.concatenate([w_old[h] * n_old[h] + jnp.sum(kw[h], axis=0, keepdims=True) for h in hd], axis=0)
        m_ref[d] = jnp.concatenate([jnp.broadcast_to(m_new[h], (1, m_ref.shape[2])) for h in hd], axis=0)

    @pl.when(c == n_chunks - 1)
    def _():
        cfin_ref[0] = c_ref[...]
        nfin_ref[0] = n_ref[...]
        mfin_ref[0] = m_ref[...]


def _mlstm(p, gcol, grow, states, n_seq, t_seq, row0):
    n_chunks = t_seq // CHUNK
    chunk0 = row0 // CHUNK

    in_specs, args, out_specs = [], [], []
    for d in range(2):
        def rows(b, c, d=d):
            return chunk0 + b * n_chunks + _chunk_index(d, c, n_chunks)

        def gidx(b, c, d=d, rows=rows):
            return (d, rows(b, c), 0, 0)

        in_specs += [pl.BlockSpec((CHUNK, D), lambda b, c, rows=rows: (rows(b, c), 0)),
                     pl.BlockSpec((CHUNK, D), lambda b, c, rows=rows: (rows(b, c), 1)),
                     pl.BlockSpec((1, 1, CHUNK, 16), gidx),
                     pl.BlockSpec((1, 1, 16, CHUNK), gidx)]
        args += [p, p, gcol, grow]
        out_specs.append(pl.BlockSpec((CHUNK, D), lambda b, c, d=d: (b * n_chunks + _chunk_index(d, c, n_chunks), 0)))
    c_spec = pl.BlockSpec((1, 2, HEADS_D, DK_D, DV_D), lambda b, c: (b, 0, 0, 0, 0))
    n_spec = pl.BlockSpec((1, 2, HEADS_D, DK_D), lambda b, c: (b, 0, 0, 0))
    m_spec = pl.BlockSpec((1, 2, HEADS_D, 128), lambda b, c: (b, 0, 0, 0))
    if states is not None:
        in_specs += [c_spec, n_spec, m_spec]
        args += list(states)
    m = n_seq * t_seq
    return pl.pallas_call(
        functools.partial(_mlstm_kernel, n_chunks=n_chunks, zero_init=states is None),
        out_shape=(jax.ShapeDtypeStruct((m, D), F32), jax.ShapeDtypeStruct((m, D), F32),
                   jax.ShapeDtypeStruct((n_seq, 2, HEADS_D, DK_D, DV_D), F32),
                   jax.ShapeDtypeStruct((n_seq, 2, HEADS_D, DK_D), F32),
                   jax.ShapeDtypeStruct((n_seq, 2, HEADS_D, 128), F32)),
        grid=(n_seq, n_chunks),
        in_specs=in_specs,
        out_specs=tuple(out_specs) + (c_spec, n_spec, m_spec),
        scratch_shapes=[pltpu.VMEM((2, HEADS_D, DK_D, DV_D), F32),
                        pltpu.VMEM((2, HEADS_D, DK_D), F32),
                        pltpu.VMEM((2, HEADS_D, 128), F32)],
        compiler_params=_cparams(2),
        name="mlstm",
    )(*args)


def _router_kernel(x_ref, sc_ref, sh_ref, wt_ref, b_ref, xs_ref, pos_ref, gate_ref, cnt_ref):
    h = (x_ref[...] * (1.0 + sc_ref[0]) + sh_ref[0]).astype(BF16)
    logits = _dot_nt(wt_ref[...], h) + b_ref[...]
    sub = lax.broadcasted_iota(jnp.int32, logits.shape, 0)
    onehots, vals = [], []
    for k in range(TOP_K):
        m = jnp.max(logits, axis=0, keepdims=True)
        idx = jnp.min(jnp.where(logits == m, sub, N_EXPERTS), axis=0, keepdims=True)
        hit = sub == idx
        logits = jnp.where(hit, -jnp.inf, logits)
        onehots.append(jnp.where(hit, 1.0, 0.0))
        vals.append(m)
    exps = [jnp.exp(v - vals[0]) for v in vals]
    den = exps[0] + exps[1] + exps[2] + exps[3]
    counts = [jnp.sum(o, axis=1, keepdims=True) for o in onehots]
    n = counts[0] + counts[1] + counts[2] + counts[3]
    n_al = jnp.floor((n + (SEG_ALIGN - 1)) * (1.0 / SEG_ALIGN)) * SEG_ALIGN
    er = lax.broadcasted_iota(jnp.int32, (N_EXPERTS, N_EXPERTS), 0)
    ec = lax.broadcasted_iota(jnp.int32, (N_EXPERTS, N_EXPERTS), 1)
    before_e = jnp.where(ec < er, 1.0, 0.0).astype(BF16)
    start = jnp.dot(before_e, jnp.broadcast_to(n_al, logits.shape).astype(BF16), preferred_element_type=F32)
    tr = lax.broadcasted_iota(jnp.int32, (TILE_TOK, TILE_TOK), 0)
    tc = lax.broadcasted_iota(jnp.int32, (TILE_TOK, TILE_TOK), 1)
    before_t = jnp.where(tr < tc, 1.0, 0.0).astype(BF16)
    rows = lax.broadcasted_iota(jnp.int32, (TILE_ROWS, TILE_TOK), 0)
    perm = jnp.zeros((TILE_ROWS, TILE_TOK), F32)
    prev = jnp.zeros_like(n)
    for k in range(TOP_K):
        cum = jnp.dot(onehots[k].astype(BF16), before_t, preferred_element_type=F32)
        pos = jnp.sum(onehots[k] * (start + prev + cum), axis=0, keepdims=True).astype(jnp.int32)
        prev = prev + counts[k]
        pos_ref[0, k:k + 1, :] = pos
        gate_ref[0, k:k + 1, :] = exps[k] / den
        perm = perm + jnp.where(rows == pos, 1.0, 0.0)
    xs_ref[...] = jnp.dot(perm.astype(BF16), h, preferred_element_type=F32)
    cnt_ref[0] = jnp.broadcast_to(n_al, cnt_ref.shape[1:]).astype(jnp.int32)


def _router(x, mod, w, b, n_prompt, t_lat):
    m = x.shape[0]
    nt = m // TILE_TOK
    cidx = functools.partial(_cond_index, tm=TILE_TOK, n_prompt=n_prompt, t_lat=t_lat)
    sel = pl.BlockSpec((1, TOP_K, TILE_TOK), lambda i: (i, 0, 0))
    return pl.pallas_call(
        _router_kernel,
        out_shape=(jax.ShapeDtypeStruct((nt * TILE_ROWS, D), F32),
                   jax.ShapeDtypeStruct((nt, TOP_K, TILE_TOK), jnp.int32),
                   jax.ShapeDtypeStruct((nt, TOP_K, TILE_TOK), F32),
                   jax.ShapeDtypeStruct((nt, N_EXPERTS, 128), jnp.int32)),
        grid=(nt,),
        in_specs=[pl.BlockSpec((TILE_TOK, D), lambda i: (i, 0)),
                  pl.BlockSpec((1, 1, D), lambda i: (cidx(i), 0, 4)),
                  pl.BlockSpec((1, 1, D), lambda i: (cidx(i), 0, 3)),
                  pl.BlockSpec((N_EXPERTS, D), lambda i: (0, 0)),
                  pl.BlockSpec((N_EXPERTS, 1), lambda i: (0, 0))],
        out_specs=(pl.BlockSpec((TILE_ROWS, D), lambda i: (i, 0)), sel, sel,
                   pl.BlockSpec((1, N_EXPERTS, 128), lambda i: (i, 0, 0))),
        compiler_params=_cparams(1),
        name="router",
    )(x, mod, mod, w.T, b.reshape(N_EXPERTS, 1))


def _expert_kernel(be_ref, boff_ref, brows_ref, tlo_ref, thi_ref, nb_ref, scum_ref, sn_ref, ssrc_ref, tused_ref,
                   xs_hbm, win_ref, bin_ref, wout_ref, bout_ref, ys_hbm,
                   xbuf, ybuf, sem_in, sem_out, winbf, woutbf, *, n_tiles):
    i = pl.program_id(0)
    nb = nb_ref[0]
    slot = i % 2

    def for_pieces(b, fn):
        e = be_ref[b]
        off = boff_ref[b]

        def body(t, carry):
            j = e * n_tiles + t
            s0 = scum_ref[j]
            lo = jnp.maximum(s0, off)
            hi = jnp.minimum(s0 + sn_ref[j], off + EXPERT_ROWS)

            @pl.when(hi > lo)
            def _():
                fn(pl.multiple_of(ssrc_ref[j] + (lo - s0), SEG_ALIGN), pl.multiple_of(lo - off, SEG_ALIGN),
                   pl.multiple_of(hi - lo, SEG_ALIGN))
            return carry
        lax.fori_loop(tlo_ref[b], thi_ref[b], body, 0)

    def in_copy(s, src, dst, n):
        return pltpu.make_async_copy(xs_hbm.at[pl.ds(src, n), :], xbuf.at[s, pl.ds(dst, n), :], sem_in.at[s])

    def out_copy(s, src, dst, n):
        return pltpu.make_async_copy(ybuf.at[s, pl.ds(dst, n), :], ys_hbm.at[pl.ds(src, n), :], sem_out.at[s])

    def wait_block(copy, b, s):
        n = pl.multiple_of(brows_ref[b], SEG_ALIGN)

        @pl.when(n > 0)
        def _():
            copy(s, 0, 0, n).wait()

    def tail_copy(t):
        used = pl.multiple_of(tused_ref[t], SEG_ALIGN)
        n = pl.multiple_of(TILE_ROWS - used, SEG_ALIGN)
        return n, pltpu.make_async_copy(ybuf.at[1, pl.ds(0, n), :],
                                        ys_hbm.at[pl.ds(pl.multiple_of(t * TILE_ROWS + used, SEG_ALIGN), n), :],
                                        sem_out.at[1])

    @pl.when(i == 0)
    def _():
        xbuf[...] = jnp.zeros(xbuf.shape, F32)
        ybuf[1] = jnp.zeros(ybuf.shape[1:], F32)
        for_pieces(0, lambda src, dst, n: in_copy(0, src, dst, n).start())

        def start_tail(t, carry):
            n, copy = tail_copy(t)

            @pl.when(n > 0)
            def _():
                copy.start()
            return carry

        def wait_tail(t, carry):
            n, copy = tail_copy(t)

            @pl.when(n > 0)
            def _():
                copy.wait()
            return carry

        lax.fori_loop(0, n_tiles, start_tail, 0)
        lax.fori_loop(0, n_tiles, wait_tail, 0)

    @pl.when(i + 1 < nb)
    def _():
        for_pieces(i + 1, lambda src, dst, n: in_copy(1 - slot, src, dst, n).start())

    new_expert = jnp.logical_or(i == 0, be_ref[i] != be_ref[jnp.maximum(i - 1, 0)])

    @pl.when(jnp.logical_and(new_expert, i < nb))
    def _():
        winbf[...] = win_ref[0].astype(BF16)
        woutbf[...] = wout_ref[0].astype(BF16)

    @pl.when(i < nb)
    def _():
        wait_block(in_copy, i, slot)
        x = xbuf[slot].astype(BF16)
        hmid = jnp.dot(x, winbf[...], preferred_element_type=F32) + bin_ref[0]
        glu = jnp.minimum(hmid[:, :D_FF], SWIGLU_LIMIT)
        lin = jnp.clip(hmid[:, D_FF:], -SWIGLU_LIMIT, SWIGLU_LIMIT)
        act = glu * _sigmoid(SWIGLU_ALPHA * glu) * (lin + 1.0)
        y = jnp.dot(act.astype(BF16), woutbf[...], preferred_element_type=F32) + bout_ref[0]

        @pl.when(i >= 1)
        def _():
            wait_block(out_copy, i - 1, 1 - slot)

        ybuf[slot] = y
        for_pieces(i, lambda src, dst, n: out_copy(slot, src, dst, n).start())

        @pl.when(i == nb - 1)
        def _():
            wait_block(out_copy, i, slot)


def _experts(xs, tables, layer, w_in, b_in, w_out, b_out, n_tiles):
    n_blocks = tables[0].shape[0]
    depth = w_in.shape[0]

    def wmap(i, be, *_):
        return (layer, be[i], 0, 0)

    grid_spec = pltpu.PrefetchScalarGridSpec(
        num_scalar_prefetch=len(tables),
        grid=(n_blocks,),
        in_specs=[pl.BlockSpec(memory_space=pl.ANY),
                  pl.BlockSpec((None, 1, D, 2 * D_FF), wmap),
                  pl.BlockSpec((None, 1, 1, 2 * D_FF), wmap),
                  pl.BlockSpec((None, 1, D_FF, D), wmap),
                  pl.BlockSpec((None, 1, 1, D), wmap)],
        out_specs=pl.BlockSpec(memory_space=pl.ANY),
        scratch_shapes=[pltpu.VMEM((2, EXPERT_ROWS, D), F32),
                        pltpu.VMEM((2, EXPERT_ROWS, D), F32),
                        pltpu.SemaphoreType.DMA((2,)),
                        pltpu.SemaphoreType.DMA((2,)),
                        pltpu.VMEM((D, 2 * D_FF), BF16),
                        pltpu.VMEM((D_FF, D), BF16)])
    return pl.pallas_call(
        functools.partial(_expert_kernel, n_tiles=n_tiles),
        out_shape=jax.ShapeDtypeStruct(xs.shape, F32),
        grid_spec=grid_spec,
        compiler_params=_cparams(1),
        name="experts",
    )(*tables, xs, w_in, b_in.reshape(depth, N_EXPERTS, 1, 2 * D_FF), w_out, b_out.reshape(depth, N_EXPERTS, 1, D))


def _combine_kernel(ys_ref, pos_ref, gate_ref, x_ref, g_ref, lg_ref, lb_ref, o_ref):
    rows = lax.broadcasted_iota(jnp.int32, (TILE_ROWS, TILE_TOK), 0)
    perm = jnp.zeros((TILE_ROWS, TILE_TOK), F32)
    gmat = jnp.zeros((TILE_ROWS, TILE_TOK), F32)
    for k in range(TOP_K):
        hit = rows == pos_ref[0, k:k + 1, :]
        perm = perm + jnp.where(hit, 1.0, 0.0)
        gmat = gmat + jnp.where(hit, gate_ref[0, k:k + 1, :], 0.0)
    row_gate = jnp.sum(gmat, axis=1, keepdims=True)
    z = ys_ref[...] * row_gate
    hi, lo = _split2(z)
    p = perm.astype(BF16)
    dims = (((0,), (0,)), ((), ()))
    y = (lax.dot_general(p, hi, dims, preferred_element_type=F32)
         + lax.dot_general(p, lo, dims, preferred_element_type=F32))
    o_ref[...] = _layer_norm(_ALPHA * x_ref[...] + g_ref[0] * y, lg_ref[...], lb_ref[...])


def _combine(ys, pos, gates, x, mod, ln_g, ln_b, n_prompt, t_lat):
    m = x.shape[0]
    cidx = functools.partial(_cond_index, tm=TILE_TOK, n_prompt=n_prompt, t_lat=t_lat)
    row = pl.BlockSpec((TILE_TOK, D), lambda i: (i, 0))
    vec = pl.BlockSpec((1, D), lambda i: (0, 0))
    sel = pl.BlockSpec((1, TOP_K, TILE_TOK), lambda i: (i, 0, 0))
    return pl.pallas_call(
        _combine_kernel,
        out_shape=jax.ShapeDtypeStruct((m, D), F32),
        grid=(m // TILE_TOK,),
        in_specs=[pl.BlockSpec((TILE_ROWS, D), lambda i: (i, 0)), sel, sel, row,
                  pl.BlockSpec((1, 1, D), lambda i: (cidx(i), 0, 5)), vec, vec],
        out_specs=row,
        compiler_params=_cparams(1),
        name="combine",
    )(ys, pos, gates, x, mod, ln_g.reshape(1, D), ln_b.reshape(1, D))


def _routing_tables(seg_n):
    n_tiles = seg_n.shape[0]
    seg_src = (jnp.arange(n_tiles, dtype=jnp.int32)[:, None] * TILE_ROWS
               + jnp.cumsum(seg_n, axis=1) - seg_n)
    seg_cum = jnp.cumsum(seg_n, axis=0) - seg_n
    count = jnp.sum(seg_n, axis=0)
    padded = (count + EXPERT_ROWS - 1) // EXPERT_ROWS * EXPERT_ROWS
    pad_end = jnp.cumsum(padded)
    pad_start = pad_end - padded
    n_blocks = n_tiles * TILE_ROWS // EXPERT_ROWS + N_EXPERTS
    n_used = pad_end[-1] // EXPERT_ROWS
    blocks = jnp.arange(n_blocks, dtype=jnp.int32)
    owner = jnp.minimum(jnp.sum(blocks[:, None] * EXPERT_ROWS >= pad_end[None, :], axis=1), N_EXPERTS - 1)
    last_owner = jnp.minimum(jnp.sum((n_used - 1) * EXPERT_ROWS >= pad_end), N_EXPERTS - 1)
    block_expert = jnp.where(blocks < n_used, owner, last_owner).astype(jnp.int32)
    own = block_expert[:, None] == jnp.arange(N_EXPERTS, dtype=jnp.int32)[None, :]
    block_off = blocks * EXPERT_ROWS - jnp.sum(jnp.where(own, pad_start[None, :], 0), axis=1)
    block_rows = jnp.clip(jnp.sum(jnp.where(own, count[None, :], 0), axis=1) - block_off, 0, EXPERT_ROWS)
    cum_b = jnp.sum(jnp.where(own[:, None, :], seg_cum[None], 0), axis=2)
    n_b = jnp.sum(jnp.where(own[:, None, :], seg_n[None], 0), axis=2)
    t_lo = jnp.sum(cum_b + n_b <= block_off[:, None], axis=1)
    t_hi = jnp.sum(cum_b < block_off[:, None] + EXPERT_ROWS, axis=1)
    i32 = lambda a: a.astype(jnp.int32)
    return (block_expert, i32(block_off), i32(block_rows), i32(t_lo), i32(t_hi), i32(n_used).reshape(1),
            i32(seg_cum.T.reshape(-1)), i32(seg_n.T.reshape(-1)), i32(seg_src.T.reshape(-1)),
            i32(jnp.sum(seg_n, axis=1)))


def _moe(x, mod, layer, w_router, b_router, w_in, b_in, w_out, b_out, ln_g, ln_b, n_prompt, t_lat):
    xs, pos, gates, cnt = _router(x, mod, w_router, b_router, n_prompt, t_lat)
    tables = _routing_tables(cnt[:, :, 0])
    ys = _experts(xs, tables, layer, w_in, b_in, w_out, b_out, cnt.shape[0])
    return _combine(ys, pos, gates, x, mod, ln_g, ln_b, n_prompt, t_lat)


def kernel(x_prompt, x_sample, cache_k_a, cache_v_a, state_b, cache_k_c, cache_v_c, state_d_c, state_d_n, state_d_m, c, c_ctx, w_mod, b_mod, ln_g, ln_b, w_in_a, q_norm_a, k_norm_a, w_out_a, w_in_b, w_conv_b, w_gate_b, a_log_b, dt_bias_b, o_norm_b, w_out_b, w_in_c, rpb_c, w_out_c, w_in_d, w_gate_d, b_gate_d, o_norm_d, w_out_d, w_router, b_router, w_exp_in, b_exp_in, w_exp_out, b_exp_out):
    bp, t_p, _ = x_prompt.shape
    bs, t_s, _ = x_sample.shape
    past = cache_k_a.shape[2]
    depth = w_mod.shape[0]
    assert depth == 4 and bs + 1 <= N_COND
    n_p, n_s = bp * t_p, bs * t_s
    x = jnp.concatenate([x_prompt.reshape(n_p, D), x_sample.reshape(n_s, D)], axis=0)
    cond = jnp.concatenate([c_ctx.reshape(1, D), c, jnp.zeros((N_COND - 1 - bs, D), F32)], axis=0)
    res = functools.partial(_proj_res_ln, n_prompt=n_p, t_lat=t_s)
    outs = {}
    for i in range(depth):
        kind, j = i % 4, i // 4
        mod = _modulation_table(cond, w_mod, b_mod[i], i).reshape(N_COND, 1, 6 * D)
        proj = functools.partial(_modmm, x, mod, 1, 0, n_prompt=n_p, t_lat=t_s)
        if kind == 0:
            p = proj(w_in_a[j])
            q, k = _prep_a(p, q_norm_a[j], k_norm_a[j], n_p, t_s)
            hq, hk = HEADS_A * DH_A, KV_A * DH_A
            vcol = hq // hk + 1
            o_p = _attention(q, pl.BlockSpec((t_p, hq), lambda b: (b, 0)),
                             [(k, pl.BlockSpec((t_p, hk), lambda b: (b, 0)),
                               p, pl.BlockSpec((t_p, hk), lambda b: (b, vcol)))],
                             n_p, (bp,), HEADS_A, KV_A, DH_A, pl.BlockSpec((t_p, hq), lambda b: (b, 0)))
            tq = 256
            qb0, lat0 = n_p // tq, n_p // t_s
            kc = cache_k_a[:, j].reshape(bs * past, hk)
            vc = cache_v_a[:, j].reshape(bs * past, hk)
            o_s = _attention(q, pl.BlockSpec((tq, hq), lambda b, t: (qb0 + b * (t_s // tq) + t, 0)),
                             [(kc, pl.BlockSpec((past, hk), lambda b, t: (b, 0)),
                               vc, pl.BlockSpec((past, hk), lambda b, t: (b, 0))),
                              (k, pl.BlockSpec((t_s, hk), lambda b, t: (lat0 + b, 0)),
                               p, pl.BlockSpec((t_s, hk), lambda b, t: (lat0 + b, vcol)))],
                             n_s, (bs, t_s // tq), HEADS_A, KV_A, DH_A,
                             pl.BlockSpec((tq, hq), lambda b, t: (b * (t_s // tq) + t, 0)))
            outs["k_a"] = k[:n_p].reshape(bp, 1, t_p, KV_A, DH_A)
            outs["v_a"] = p[:n_p, hq + hk:].reshape(bp, 1, t_p, KV_A, DH_A)
            x = res("plain", (o_p, o_s), w_out_a[j], x, mod, 2, ln_g[i, 0], ln_b[i, 0])
        elif kind == 1:
            p = proj(w_in_b[j])
            raw = proj(w_gate_b[j])
            zeros8 = jnp.zeros((2, HEADS_B), F32)
            a32 = jnp.concatenate([a_log_b[j], zeros8], axis=1).reshape(1, 32)
            dt32 = jnp.concatenate([dt_bias_b[j], zeros8], axis=1).reshape(1, 32)
            gcol, grow = _gate_layouts(_gates("gdn", raw, a32, dt32))
            qp, kp, vp = _prep_b(p, w_conv_b[j], 0, n_p, t_p)
            qs, ks, vs = _prep_b(p, w_conv_b[j], n_p, n_s, t_s)
            o_pf, o_pb, s_p = _gdn(qp, kp, vp, gcol, grow, None, bp, t_p, 0)
            o_sf, o_sb, _ = _gdn(qs, ks, vs, gcol, grow, state_b[:, j], bs, t_s, n_p // CHUNK)
            outs["s_b"] = s_p.reshape(bp, 1, 2, HEADS_B, DK_B, DK_B)
            x = res("silu", (o_pf, o_pb, o_sf, o_sb, p, o_norm_b[j].reshape(1, DK_B)), w_out_b[j], x, mod, 2, ln_g[i, 0], ln_b[i, 0],
                    z_colblock=3)
        elif kind == 2:
            hd = HEADS_C * DH_C
            pq, pk, pv = (proj(w_in_c[j][:, s * hd:(s + 1) * hd]) for s in range(3))
            blk = pl.BlockSpec((t_p, hd), lambda b: (b, 0))
            o_p = _attention(pq, blk, [(pk, blk, pv, blk)], n_p, (bp,), HEADS_C, HEADS_C, DH_C, blk)
            o_s = _nbr_attention(pq, pk, pv, cache_k_c[:, j].reshape(bs * past, hd),
                                 cache_v_c[:, j].reshape(bs * past, hd), rpb_c[j], n_p, bs, t_s, past)
            outs["k_c"] = pk[:n_p].reshape(bp, 1, t_p, HEADS_C, DH_C)
            outs["v_c"] = pv[:n_p].reshape(bp, 1, t_p, HEADS_C, DH_C)
            x = res("plain", (o_p, o_s), w_out_c[j], x, mod, 2, ln_g[i, 0], ln_b[i, 0])
        else:
            p = proj(w_in_d[j])
            raw = proj(w_gate_d[j])
            gcol, grow = _gate_layouts(_gates("mlstm", raw, b_gate_d[j].reshape(1, 32), jnp.zeros((1, 32), F32)))
            h_pf, h_pb, c_p, n_fin, m_fin = _mlstm(p, gcol, grow, None, bp, t_p, 0)
            m0 = jnp.broadcast_to(state_d_m[:, j][..., None], (bs, 2, HEADS_D, 128))
            h_sf, h_sb, _, _, _ = _mlstm(p, gcol, grow, (state_d_c[:, j], state_d_n[:, j], m0), bs, t_s, n_p)
            outs["c_d"] = c_p.reshape(bp, 1, 2, HEADS_D, DK_D, DV_D)
            outs["n_d"] = n_fin.reshape(bp, 1, 2, HEADS_D, DK_D)
            outs["m_d"] = m_fin[..., 0].reshape(bp, 1, 2, HEADS_D)
            x = res("sigmoid", (h_pf, h_pb, h_sf, h_sb, p, o_norm_d[j].reshape(1, DV_D)), w_out_d[j], x, mod, 2, ln_g[i, 0], ln_b[i, 0],
                    z_colblock=2)
        x = _moe(x, mod, i, w_router[i], b_router[i], w_exp_in, b_exp_in, w_exp_out, b_exp_out,
                 ln_g[i, 1], ln_b[i, 1], n_p, t_s)
    return (x[:n_p].reshape(bp, t_p, D), x[n_p:].reshape(bs, t_s, D), outs["k_a"], outs["v_a"], outs["s_b"],
            outs["k_c"], outs["v_c"], outs["c_d"], outs["n_d"], outs["m_d"])
```

```python
import functools

import jax
import jax.numpy as jnp
import numpy as np
from jax import lax
from jax.experimental import pallas as pl
from jax.experimental.pallas import tpu as pltpu

F32 = jnp.float32
BF16 = jnp.bfloat16

D = 1024
GRID_W = 64
HEADS_A, KV_A, DH_A = 16, 4, 64
ROPE_THETA = 10000.0
HEADS_B, DK_B = 8, 128
CONV_W = 3
CHUNK = 64
HEADS_C, DH_C = 16, 64
WIN_R, WIN_C = 8, 16
HEADS_D, DK_D, DV_D = 8, 64, 128
N_EXPERTS, TOP_K, D_FF = 32, 4, 1024
SWIGLU_LIMIT, SWIGLU_ALPHA = 7.0, 1.702
NEG_INF = -1e30
N_COND = 8
EXPERT_ROWS = 512
TILE_TOK = 256
SEG_ALIGN = 8
TILE_ROWS = (TILE_TOK * TOP_K + N_EXPERTS * (SEG_ALIGN - 1) + 7) // 8 * 8
VMEM_LIMIT = 56 * 1024 * 1024
_ALPHA = float((2 * 4) ** 0.25)


def _cparams(n_axes):
    return pltpu.CompilerParams(dimension_semantics=("arbitrary",) * n_axes, vmem_limit_bytes=VMEM_LIMIT)


def _dot(a, b):
    return jnp.dot(a.astype(BF16), b.astype(BF16), preferred_element_type=F32)


def _dot_nt(a, b):
    return lax.dot_general(a.astype(BF16), b.astype(BF16), (((1,), (1,)), ((), ())), preferred_element_type=F32)


def _dot_tn(a, b):
    return lax.dot_general(a.astype(BF16), b.astype(BF16), (((0,), (0,)), ((), ())), preferred_element_type=F32)


def _split2(x):
    hi = x.astype(BF16)
    lo = (x - hi.astype(F32)).astype(BF16)
    return hi, lo


def _split3(x):
    hi = x.astype(BF16)
    r = x - hi.astype(F32)
    mid = r.astype(BF16)
    lo = (r - mid.astype(F32)).astype(BF16)
    return hi, mid, lo


def _dot_hi(a, b):
    ah, al = _split2(a)
    bh, bl = _split2(b)
    d = functools.partial(jnp.dot, preferred_element_type=F32)
    return d(ah, bh) + (d(ah, bl) + d(al, bh))


def _mask_dot(mask_bf16, x):
    d = functools.partial(jnp.dot, preferred_element_type=F32)
    hi, mid, lo = _split3(x)
    return d(mask_bf16, hi) + (d(mask_bf16, mid) + d(mask_bf16, lo))


def _dot_mask(x, mask_bf16):
    d = functools.partial(jnp.dot, preferred_element_type=F32)
    hi, mid, lo = _split3(x)
    return d(hi, mask_bf16) + (d(mid, mask_bf16) + d(lo, mask_bf16))


def _layer_norm(x, g, b):
    mu = jnp.mean(x, axis=-1, keepdims=True)
    xc = x - mu
    var = jnp.mean(xc * xc, axis=-1, keepdims=True)
    return xc * lax.rsqrt(var + 1e-5) * g + b


def _sigmoid(x):
    return 1.0 / (1.0 + jnp.exp(-x))


def _cond_index(i, tm, n_prompt, t_lat):
    npt = n_prompt // tm
    return jnp.where(i < npt, 0, 1 + (i - npt) // (t_lat // tm))


def _mod_kernel(c_ref, w_ref, b_ref, o_ref):
    c = c_ref[...]
    o_ref[...] = _dot(c * _sigmoid(c), w_ref[...]) + b_ref[...]


def _modulation_table(cond, w, b, layer):
    n = w.shape[2]
    tn = 1024
    return pl.pallas_call(
        _mod_kernel,
        out_shape=jax.ShapeDtypeStruct((N_COND, n), F32),
        grid=(n // tn,),
        in_specs=[pl.BlockSpec((N_COND, D), lambda j: (0, 0)),
                  pl.BlockSpec((None, D, tn), lambda j: (layer, 0, j)),
                  pl.BlockSpec((1, tn), lambda j: (0, j))],
        out_specs=pl.BlockSpec((N_COND, tn), lambda j: (0, j)),
        compiler_params=_cparams(1),
        name="mod_table",
    )(cond, w, b.reshape(1, n))


def _modmm_kernel(x_ref, sc_ref, sh_ref, w_ref, o_ref, wbf_ref):
    @pl.when(pl.program_id(1) == 0)
    def _():
        wbf_ref[...] = w_ref[...].astype(BF16)

    h = x_ref[...] * (1.0 + sc_ref[0]) + sh_ref[0]
    o_ref[...] = jnp.dot(h.astype(BF16), wbf_ref[...], preferred_element_type=F32)


def _col_tile(n):
    for t in (2048, 1536, 1024, 512, 256, 128):
        if n % t == 0:
            return t
    return n


def _modmm(x, mod, which_sc, which_sh, w, n_prompt, t_lat, tm=512):
    m = x.shape[0]
    n = w.shape[1]
    tn = _col_tile(n)
    cidx = functools.partial(_cond_index, tm=tm, n_prompt=n_prompt, t_lat=t_lat)
    return pl.pallas_call(
        _modmm_kernel,
        out_shape=jax.ShapeDtypeStruct((m, n), F32),
        grid=(n // tn, m // tm),
        in_specs=[pl.BlockSpec((tm, D), lambda j, i: (i, 0)),
                  pl.BlockSpec((1, 1, D), lambda j, i: (cidx(i), 0, which_sc)),
                  pl.BlockSpec((1, 1, D), lambda j, i: (cidx(i), 0, which_sh)),
                  pl.BlockSpec((D, tn), lambda j, i: (0, j))],
        out_specs=pl.BlockSpec((tm, tn), lambda j, i: (i, j)),
        scratch_shapes=[pltpu.VMEM((D, tn), BF16)],
        compiler_params=_cparams(2),
        name="mod_proj",
    )(x, mod, mod, w)


def _proj_kernel(*refs, kind, prompt_tiles):
    if kind == "plain":
        ap_ref, as_ref, w_ref, x_ref, g_ref, lg_ref, lb_ref, o_ref, wbf_ref = refs
    else:
        (pf_ref, pb_ref, sf_ref, sb_ref, z_ref, on_ref, w_ref, x_ref, g_ref, lg_ref, lb_ref,
         o_ref, wbf_ref, abuf_ref) = refs
    is_prompt = pl.program_id(0) < prompt_tiles

    @pl.when(pl.program_id(0) == 0)
    def _():
        wbf_ref[...] = w_ref[...].astype(BF16)

    if kind == "plain":
        a = jnp.where(is_prompt, ap_ref[...], as_ref[...]).astype(BF16)
    else:
        s = jnp.where(is_prompt, pf_ref[...] + pb_ref[...], sf_ref[...] + sb_ref[...])
        on = on_ref[...]
        hd = on.shape[-1]
        for h in range(D // hd):
            sl = slice(h * hd, (h + 1) * hd)
            sh = s[:, sl]
            y = sh * lax.rsqrt(jnp.mean(sh * sh, axis=-1, keepdims=True) + 1e-6) * on
            z = z_ref[:, sl]
            gate = _sigmoid(z)
            if kind == "silu":
                gate = z * gate
            abuf_ref[:, sl] = (y * gate).astype(BF16)
        a = abuf_ref[...]
    y = jnp.dot(a, wbf_ref[...], preferred_element_type=F32)
    o_ref[...] = _layer_norm(_ALPHA * x_ref[...] + g_ref[0] * y, lg_ref[...], lb_ref[...])


def _proj_res_ln(kind, a_args, w, x, mod, which_g, ln_g, ln_b, n_prompt, t_lat, z_colblock=0, tm=512):
    m = x.shape[0]
    cidx = functools.partial(_cond_index, tm=tm, n_prompt=n_prompt, t_lat=t_lat)
    row = pl.BlockSpec((tm, D), lambda i: (i, 0))
    vec = pl.BlockSpec((1, D), lambda i: (0, 0))
    npt = n_prompt // tm
    prompt_rows = pl.BlockSpec((tm, D), lambda i: (jnp.minimum(i, npt - 1), 0))
    latent_rows = pl.BlockSpec((tm, D), lambda i: (jnp.maximum(i - npt, 0), 0))
    if kind == "plain":
        a_specs = [prompt_rows, latent_rows]
        scratch = [pltpu.VMEM((D, D), BF16)]
    else:
        on = a_args[5]
        a_specs = [prompt_rows, prompt_rows, latent_rows, latent_rows,
                   pl.BlockSpec((tm, D), lambda i: (i, z_colblock)),
                   pl.BlockSpec((1, on.shape[-1]), lambda i: (0, 0))]
        scratch = [pltpu.VMEM((D, D), BF16), pltpu.VMEM((tm, D), BF16)]
    return pl.pallas_call(
        functools.partial(_proj_kernel, kind=kind, prompt_tiles=npt),
        out_shape=jax.ShapeDtypeStruct((m, D), F32),
        grid=(m // tm,),
        in_specs=a_specs + [pl.BlockSpec((D, D), lambda i: (0, 0)), row,
                            pl.BlockSpec((1, 1, D), lambda i: (cidx(i), 0, which_g)), vec, vec],
        out_specs=row,
        scratch_shapes=scratch,
        compiler_params=_cparams(1),
        name="out_proj_" + kind,
    )(*a_args, w, x, mod, ln_g.reshape(1, D), ln_b.reshape(1, D))


def _prep_a_kernel(p_ref, qn_ref, kn_ref, cos_ref, sin_ref, q_ref, k_ref):
    cos = cos_ref[...]
    sin = sin_ref[...]
    width = 2 * DH_A
    lane = lax.broadcasted_iota(jnp.int32, cos.shape, 1)
    first_half = lane % DH_A < DH_A // 2
    r = lax.broadcasted_iota(jnp.int32, (width, width), 0)
    c = lax.broadcasted_iota(jnp.int32, (width, width), 1)
    same_head = jnp.where(r // DH_A == c // DH_A, 1.0, 0.0).astype(BF16)

    def pair(x, g):
        ms = _dot_mask(x * x, same_head) * (1.0 / DH_A)
        y = x * lax.rsqrt(ms + 1e-6) * g
        swapped = jnp.where(first_half, pltpu.roll(y, width - DH_A // 2, axis=1), pltpu.roll(y, DH_A // 2, axis=1))
        return y * cos + swapped * sin

    for j in range(HEADS_A // 2):
        sl = slice(j * width, (j + 1) * width)
        q_ref[:, sl] = pair(p_ref[:, sl], qn_ref[...])
    for j in range(KV_A // 2):
        sl = slice(j * width, (j + 1) * width)
        k_ref[:, sl] = pair(p_ref[:, HEADS_A * DH_A + j * width: HEADS_A * DH_A + (j + 1) * width], kn_ref[...])


def _rope_tables(t_lat, tm):
    quarter = DH_A // 4
    t = np.arange(t_lat)
    inv_freq = ROPE_THETA ** (-np.arange(quarter, dtype=np.float32) / quarter)
    ang = np.concatenate([(t // GRID_W)[:, None] * inv_freq, (t % GRID_W)[:, None] * inv_freq], axis=-1)
    ang = jnp.asarray(ang, F32)
    cos, sin = jnp.cos(ang), jnp.sin(ang)
    cos64 = jnp.concatenate([cos, cos], axis=-1)
    sin64 = jnp.concatenate([-sin, sin], axis=-1)
    cos64 = jnp.concatenate([jnp.ones((tm, DH_A), F32), cos64], axis=0)
    sin64 = jnp.concatenate([jnp.zeros((tm, DH_A), F32), sin64], axis=0)
    return jnp.tile(cos64, (1, 2)), jnp.tile(sin64, (1, 2))


def _prep_a(p, q_norm, k_norm, n_prompt, t_lat, tm=256):
    m = p.shape[0]
    cos64, sin64 = _rope_tables(t_lat, tm)
    npt = n_prompt // tm

    def tab(i):
        return (jnp.where(i < npt, 0, 1 + (i - npt) % (t_lat // tm)), 0)

    return pl.pallas_call(
        _prep_a_kernel,
        out_shape=(jax.ShapeDtypeStruct((m, HEADS_A * DH_A), F32), jax.ShapeDtypeStruct((m, KV_A * DH_A), F32)),
        grid=(m // tm,),
        in_specs=[pl.BlockSpec((tm, p.shape[1]), lambda i: (i, 0)),
                  pl.BlockSpec((1, 2 * DH_A), lambda i: (0, 0)),
                  pl.BlockSpec((1, 2 * DH_A), lambda i: (0, 0)),
                  pl.BlockSpec((tm, 2 * DH_A), tab),
                  pl.BlockSpec((tm, 2 * DH_A), tab)],
        out_specs=(pl.BlockSpec((tm, HEADS_A * DH_A), lambda i: (i, 0)),
                   pl.BlockSpec((tm, KV_A * DH_A), lambda i: (i, 0))),
        compiler_params=_cparams(1),
        name="prep_a",
    )(p, jnp.tile(q_norm.reshape(1, DH_A), (1, 2)), jnp.tile(k_norm.reshape(1, DH_A), (1, 2)), cos64, sin64)


def _softmax_pv(scores, values):
    m = scores[0].max(axis=-1, keepdims=True)
    for s in scores[1:]:
        m = jnp.maximum(m, s.max(axis=-1, keepdims=True))
    ps = [jnp.exp(s - m) for s in scores]
    den = ps[0].sum(axis=-1, keepdims=True)
    for p in ps[1:]:
        den = den + p.sum(axis=-1, keepdims=True)
    o = _dot(ps[0], values[0])
    for p, v in zip(ps[1:], values[1:]):
        o = o + _dot(p, v)
    return o * (1.0 / den)


def _pipelined_heads(n_heads, score_fn, finish_fn, batch=4):
    pending = []
    for h0 in range(0, n_heads, batch):
        current = [(h, score_fn(h)) for h in range(h0, min(h0 + batch, n_heads))]
        for h, sc in pending:
            finish_fn(h, sc)
        pending = current
    for h, sc in pending:
        finish_fn(h, sc)


def _attn_kernel(*refs, heads, kv_heads, dh, nseg):
    q_ref = refs[0]
    kv = refs[1:1 + 2 * nseg]
    o_ref = refs[1 + 2 * nseg]
    scale = dh ** -0.5
    group = heads // kv_heads

    def scores(h):
        ksl = slice(h // group * dh, (h // group + 1) * dh)
        q = q_ref[:, h * dh:(h + 1) * dh] * scale
        return [_dot_nt(q, kv[2 * s][:, ksl]) for s in range(nseg)]

    def finish(h, sc):
        ksl = slice(h // group * dh, (h // group + 1) * dh)
        o_ref[:, h * dh:(h + 1) * dh] = _softmax_pv(sc, [kv[2 * s + 1][:, ksl] for s in range(nseg)])

    _pipelined_heads(heads, scores, finish)


def _attention(q_arr, q_spec, segs, out_rows, grid, heads, kv_heads, dh, out_spec):
    in_specs = [q_spec]
    args = [q_arr]
    for (k_arr, k_spec, v_arr, v_spec) in segs:
        in_specs += [k_spec, v_spec]
        args += [k_arr, v_arr]
    return pl.pallas_call(
        functools.partial(_attn_kernel, heads=heads, kv_heads=kv_heads, dh=dh, nseg=len(segs)),
        out_shape=jax.ShapeDtypeStruct((out_rows, heads * dh), F32),
        grid=grid,
        in_specs=in_specs,
        out_specs=out_spec,
        compiler_params=_cparams(len(grid)),
        name="attention",
    )(*args)


def _window_start(r, n_rows):
    wr = min(WIN_R, n_rows)
    return jnp.minimum(jnp.maximum(r - wr // 2, 0), n_rows - wr)


def _nbr_kernel(q_ref, kl_ref, vl_ref, kc_ref, vc_ref, bias_ref, o_ref, *, n_rows):
    r = pl.program_id(1)
    wr = min(WIN_R, n_rows)
    start = pl.multiple_of(_window_start(r, n_rows) * GRID_W, GRID_W)
    scale = DH_C ** -0.5
    window = pl.ds(start, wr * GRID_W)

    def scores(h):
        sl = slice(h * DH_C, (h + 1) * DH_C)
        q = q_ref[:, sl] * scale
        return [_dot_nt(q, kl_ref[window, sl]) + bias_ref[0, h], _dot_nt(q, kc_ref[:, sl])]

    def finish(h, sc):
        sl = slice(h * DH_C, (h + 1) * DH_C)
        o_ref[:, sl] = _softmax_pv(sc, [vl_ref[window, sl], vc_ref[:, sl]])

    _pipelined_heads(HEADS_C, scores, finish)


def _nbr_bias(rpb, n_rows):
    wr = min(WIN_R, n_rows)
    cols = np.arange(GRID_W)
    col_start = np.clip(cols - WIN_C // 2, 0, GRID_W - WIN_C)
    col_ok = (cols[None, :] >= col_start[:, None]) & (cols[None, :] < col_start[:, None] + WIN_C)
    dc = np.clip(cols[None, :] - cols[:, None], -(WIN_C - 1), WIN_C - 1) + (WIN_C - 1)
    n_off = 2 * WIN_R - wr
    pick = np.zeros((2 * WIN_C - 1, GRID_W * GRID_W), np.float32)
    pick[dc.reshape(-1), np.arange(GRID_W * GRID_W)] = 1.0
    tab = jnp.einsum("hrd,dx->hrx", rpb.astype(F32), jnp.asarray(pick), precision=lax.Precision.HIGHEST)
    tab = jnp.where(col_ok[None, None], tab.reshape(HEADS_C, 2 * WIN_R - 1, GRID_W, GRID_W), NEG_INF)
    b = jnp.stack([tab[:, off:off + wr] for off in range(n_off)], axis=0)
    return jnp.transpose(b, (0, 1, 3, 2, 4)).reshape(n_off, HEADS_C, GRID_W, wr * GRID_W)


def _nbr_attention(q, k, v, k_ctx, v_ctx, rpb, n_prompt, n_lat_b, t_lat, past):
    n_rows = t_lat // GRID_W
    wr = min(WIN_R, n_rows)
    bias = _nbr_bias(rpb, n_rows)
    row0 = n_prompt // GRID_W
    lat0 = n_prompt // t_lat
    hd = HEADS_C * DH_C

    def off(b, r):
        return (_window_start(r, n_rows) - r + (WIN_R - 1), 0, 0, 0)

    return pl.pallas_call(
        functools.partial(_nbr_kernel, n_rows=n_rows),
        out_shape=jax.ShapeDtypeStruct((n_lat_b * t_lat, hd), F32),
        grid=(n_lat_b, n_rows),
        in_specs=[pl.BlockSpec((GRID_W, hd), lambda b, r: (row0 + b * n_rows + r, 0)),
                  pl.BlockSpec((t_lat, hd), lambda b, r: (lat0 + b, 0)),
                  pl.BlockSpec((t_lat, hd), lambda b, r: (lat0 + b, 0)),
                  pl.BlockSpec((past, hd), lambda b, r: (b, 0)),
                  pl.BlockSpec((past, hd), lambda b, r: (b, 0)),
                  pl.BlockSpec((1, HEADS_C, GRID_W, wr * GRID_W), off)],
        out_specs=pl.BlockSpec((GRID_W, hd), lambda b, r: (b * n_rows + r, 0)),
        compiler_params=_cparams(2),
        name="nbr_attention",
    )(q, k, v, k_ctx, v_ctx, bias)


def _chunk_masks(d):
    r = lax.broadcasted_iota(jnp.int32, (CHUNK, CHUNK), 0)
    c = lax.broadcasted_iota(jnp.int32, (CHUNK, CHUNK), 1)
    diff = r - c if d == 0 else c - r
    incl = diff >= 0
    strict = diff > 0
    incl_t = diff <= 0
    return incl, strict, incl_t


def _gate_kernel(raw_ref, p0_ref, p1_ref, o_ref, *, kind):
    raw = raw_ref[...]
    lane = lax.broadcasted_iota(jnp.int32, raw.shape, 1)
    first = (lane % 16) < 8
    if kind == "gdn":
        g = -jnp.exp(p0_ref[...]) * jax.nn.softplus(raw + p1_ref[...])
        o_ref[...] = jnp.where(first, g, _sigmoid(raw))
    else:
        x = raw + p0_ref[...]
        o_ref[...] = jnp.where(first, x, -jax.nn.softplus(-x))


def _gates(kind, raw, p0, p1, tm=512):
    m, n = raw.shape
    return pl.pallas_call(
        functools.partial(_gate_kernel, kind=kind),
        out_shape=jax.ShapeDtypeStruct((m, n), F32),
        grid=(m // tm,),
        in_specs=[pl.BlockSpec((tm, n), lambda i: (i, 0)),
                  pl.BlockSpec((1, n), lambda i: (0, 0)),
                  pl.BlockSpec((1, n), lambda i: (0, 0))],
        out_specs=pl.BlockSpec((tm, n), lambda i: (i, 0)),
        compiler_params=_cparams(1),
        name="gates_" + kind,
    )(raw, p0, p1)


def _gate_layouts(g):
    m = g.shape[0]
    col = g.reshape(m // CHUNK, CHUNK, 2, 16).transpose(2, 0, 1, 3)
    return col, col.transpose(0, 1, 3, 2)


def _prep_b_kernel(p_ref, prev_ref, next_ref, w_ref, q_ref, k_ref, v_ref, *, tm, tiles_per_seq):
    i = pl.program_id(0)
    pos = i % tiles_per_seq
    rows = lax.broadcasted_iota(jnp.int32, (tm, D), 0)
    outs = (q_ref, k_ref, v_ref)
    for seg in range(3):
        sl = slice(seg * D, (seg + 1) * D)
        x = p_ref[:, sl]
        prev_row = jnp.where(pos == 0, 0.0, prev_ref[7:8, sl])
        next_row = jnp.where(pos == tiles_per_seq - 1, 0.0, next_ref[0:1, sl])
        x_prev = jnp.where(rows == 0, prev_row, pltpu.roll(x, 1, axis=0))
        x_next = jnp.where(rows == tm - 1, next_row, pltpu.roll(x, tm - 1, axis=0))
        y = x_prev * w_ref[0:1, sl] + x * w_ref[1:2, sl] + x_next * w_ref[2:3, sl]
        y = y * _sigmoid(y)
        if seg < 2:
            for h in range(HEADS_B):
                hs = slice(h * DK_B, (h + 1) * DK_B)
                yh = y[:, hs]
                outs[seg][:, hs] = yh * lax.rsqrt(jnp.sum(yh * yh, axis=-1, keepdims=True) + 1e-6)
        else:
            v_ref[...] = y


def _prep_b(p, w_conv, row0, n_rows, t_seq, tm=256):
    total8 = p.shape[0] // 8
    r0 = row0 // tm
    tiles_per_seq = t_seq // tm
    out = jax.ShapeDtypeStruct((n_rows, D), F32)
    ospec = pl.BlockSpec((tm, D), lambda i: (i, 0))
    return pl.pallas_call(
        functools.partial(_prep_b_kernel, tm=tm, tiles_per_seq=tiles_per_seq),
        out_shape=(out, out, out),
        grid=(n_rows // tm,),
        in_specs=[pl.BlockSpec((tm, p.shape[1]), lambda i: (r0 + i, 0)),
                  pl.BlockSpec((8, p.shape[1]), lambda i: (jnp.maximum((r0 + i) * (tm // 8) - 1, 0), 0)),
                  pl.BlockSpec((8, p.shape[1]), lambda i: (jnp.minimum((r0 + i + 1) * (tm // 8), total8 - 1), 0)),
                  pl.BlockSpec((CONV_W, 3 * D), lambda i: (0, 0))],
        out_specs=(ospec, ospec, ospec),
        compiler_params=_cparams(1),
        name="prep_b",
    )(p, p, p, w_conv)


def _unit_tri_inverses(mats):
    r = lax.broadcasted_iota(jnp.int32, mats[0].shape, 0)
    c = lax.broadcasted_iota(jnp.int32, mats[0].shape, 1)
    eye = jnp.where(r == c, 1.0, 0.0)
    ps = [-a for a in mats]
    ts = [eye + p for p in ps]
    for _ in range(int(np.log2(CHUNK)) - 1):
        ps = [_dot_hi(p, p) for p in ps]
        ts = [t + _dot_hi(t, p) for t, p in zip(ts, ps)]
    return ts


def _gdn_kernel(*refs, n_chunks, zero_init):
    ins, rest = refs[:10], refs[10:]
    if not zero_init:
        s0_ref, rest = rest[0], rest[1:]
    o_refs, (sfin_ref, s_ref) = rest[:2], rest[2:]
    c = pl.program_id(1)

    @pl.when(c == 0)
    def _():
        if zero_init:
            s_ref[...] = jnp.zeros(s_ref.shape, F32)
        else:
            s_ref[...] = s0_ref[0]

    scale = DK_B ** -0.5
    qh, kh, vh, gc_c, b_c, gl, states, decay, strict = [], [], [], [], [], [], [], [], []
    for d in range(2):
        q_ref, k_ref, v_ref, gc_ref, gr_ref = ins[5 * d:5 * d + 5]
        incl, strict_d, incl_t = _chunk_masks(d)
        gcol = gc_ref[0, 0]
        grow = gr_ref[0, 0]
        gcs = _mask_dot(jnp.where(incl, 1.0, 0.0).astype(BF16), gcol)
        grs = _dot_mask(grow, jnp.where(incl_t, 1.0, 0.0).astype(BF16))
        last = CHUNK - 1 if d == 0 else 0
        q_all, k_all, v_all = q_ref[...], k_ref[...], v_ref[...]
        for h in range(HEADS_B):
            hs = slice(h * DK_B, (h + 1) * DK_B)
            qh.append(q_all[:, hs] * scale)
            kh.append(k_all[:, hs])
            vh.append(v_all[:, hs])
            gc_c.append(gcs[:, h:h + 1])
            b_c.append(gcol[:, HEADS_B + h:HEADS_B + h + 1])
            gl.append(gcs[last:last + 1, h:h + 1])
            states.append(s_ref[d, h])
            decay.append(jnp.exp(jnp.where(incl, gcs[:, h:h + 1] - grs[h:h + 1, :], NEG_INF)))
            strict.append(strict_d)
    chains = range(2 * HEADS_B)
    egc = [jnp.exp(g) for g in gc_c]
    kb = [kh[i] * b_c[i] for i in chains]
    a_mat = [jnp.where(strict[i], _dot_nt(kb[i], kh[i]) * decay[i], 0.0) for i in chains]
    attn = [_dot_nt(qh[i], kh[i]) * decay[i] for i in chains]
    o_state = [_dot(qh[i] * egc[i], states[i]) for i in chains]
    t_inv = _unit_tri_inverses(a_mat)
    u = [_dot_hi(t_inv[i], vh[i] * b_c[i]) for i in chains]
    w = [_dot_hi(t_inv[i], kb[i] * egc[i]) for i in chains]
    v_new = [u[i] - _dot(w[i], states[i]) for i in chains]
    outs = [o_state[i] + _dot(attn[i], v_new[i]) for i in chains]
    for d in range(2):
        o_refs[d][...] = jnp.concatenate(outs[d * HEADS_B:(d + 1) * HEADS_B], axis=1)
    for i in chains:
        s_ref[i // HEADS_B, i % HEADS_B] = (states[i] * jnp.exp(gl[i])
                                            + _dot_tn(kh[i] * jnp.exp(gl[i] - gc_c[i]), v_new[i]))

    @pl.when(c == n_chunks - 1)
    def _():
        sfin_ref[0] = s_ref[...]


def _chunk_index(d, c, n_chunks):
    return c + d * (n_chunks - 1 - 2 * c)


def _gdn(q, k, v, gcol, grow, s0, n_seq, t_seq, chunk0):
    n_chunks = t_seq // CHUNK
    in_specs, args, out_specs = [], [], []
    for d in range(2):
        def rows(b, c, d=d):
            return (b * n_chunks + _chunk_index(d, c, n_chunks), 0)

        def gidx(b, c, d=d):
            return (d, chunk0 + b * n_chunks + _chunk_index(d, c, n_chunks), 0, 0)

        blk = pl.BlockSpec((CHUNK, D), rows)
        in_specs += [blk, blk, blk, pl.BlockSpec((1, 1, CHUNK, 16), gidx), pl.BlockSpec((1, 1, 16, CHUNK), gidx)]
        args += [q, k, v, gcol, grow]
        out_specs.append(blk)
    state_spec = pl.BlockSpec((1, 2, HEADS_B, DK_B, DK_B), lambda b, c: (b, 0, 0, 0, 0))
    if s0 is not None:
        in_specs.append(state_spec)
        args.append(s0)
    m = n_seq * t_seq
    return pl.pallas_call(
        functools.partial(_gdn_kernel, n_chunks=n_chunks, zero_init=s0 is None),
        out_shape=(jax.ShapeDtypeStruct((m, D), F32), jax.ShapeDtypeStruct((m, D), F32),
                   jax.ShapeDtypeStruct((n_seq, 2, HEADS_B, DK_B, DK_B), F32)),
        grid=(n_seq, n_chunks),
        in_specs=in_specs,
        out_specs=tuple(out_specs) + (state_spec,),
        scratch_shapes=[pltpu.VMEM((2, HEADS_B, DK_B, DK_B), F32)],
        compiler_params=_cparams(2),
        name="gdn",
    )(*args)


def _mlstm_kernel(*refs, n_chunks, zero_init):
    ins, rest = refs[:8], refs[8:]
    if not zero_init:
        (c0_ref, n0_ref, m0_ref), rest = rest[:3], rest[3:]
    o_refs, (cfin_ref, nfin_ref, mfin_ref, c_ref, n_ref, m_ref) = rest[:2], rest[2:]
    c = pl.program_id(1)

    @pl.when(c == 0)
    def _():
        if zero_init:
            c_ref[...] = jnp.zeros(c_ref.shape, F32)
            n_ref[...] = jnp.zeros(n_ref.shape, F32)
            m_ref[...] = jnp.zeros(m_ref.shape, F32)
        else:
            c_ref[...] = c0_ref[0]
            n_ref[...] = n0_ref[0]
            m_ref[...] = m0_ref[0]

    scale = DK_D ** -0.5
    qh, kh, vh, b_c, ig_c, m_old, n_old, c_old, d_intra, b_last = [], [], [], [], [], [], [], [], [], []
    for d in range(2):
        qk_ref, v_ref, gc_ref, gr_ref = ins[4 * d:4 * d + 4]
        incl, _, incl_t = _chunk_masks(d)
        gcol = gc_ref[0, 0]
        grow = gr_ref[0, 0]
        bcs = _mask_dot(jnp.where(incl, 1.0, 0.0).astype(BF16), gcol)
        brs = _dot_mask(grow, jnp.where(incl_t, 1.0, 0.0).astype(BF16))
        last = CHUNK - 1 if d == 0 else 0
        qk_all, v_all = qk_ref[...], v_ref[...]
        m_all, n_all = m_ref[d], n_ref[d]
        for h in range(HEADS_D):
            qh.append(qk_all[:, h * DK_D:(h + 1) * DK_D] * scale)
            kh.append(qk_all[:, HEADS_D * DK_D + h * DK_D: HEADS_D * DK_D + (h + 1) * DK_D])
            vh.append(v_all[:, h * DV_D:(h + 1) * DV_D])
            b_c.append(bcs[:, HEADS_D + h:HEADS_D + h + 1])
            ig_c.append(gcol[:, h:h + 1])
            m_old.append(m_all[h:h + 1, 0:1])
            n_old.append(n_all[h:h + 1, :])
            c_old.append(c_ref[d, h])
            d_intra.append(jnp.where(incl, bcs[:, HEADS_D + h:HEADS_D + h + 1]
                                     - brs[HEADS_D + h:HEADS_D + h + 1, :] + grow[h:h + 1, :], NEG_INF))
            b_last.append(bcs[last:last + 1, HEADS_D + h:HEADS_D + h + 1])
    heads = range(2 * HEADS_D)
    a_inter = [b_c[h] + m_old[h] for h in heads]
    m_t = [jnp.maximum(a_inter[h], jnp.max(d_intra[h], axis=-1, keepdims=True)) for h in heads]
    w_inter = [jnp.exp(a_inter[h] - m_t[h]) for h in heads]
    s = [_dot_nt(qh[h], kh[h]) * jnp.exp(d_intra[h] - m_t[h]) for h in heads]
    q_state = [_dot(qh[h], c_old[h]) for h in heads]
    d_state = [b_last[h] - b_c[h] + ig_c[h] for h in heads]
    m_new = [jnp.maximum(b_last[h] + m_old[h], jnp.max(d_state[h], axis=0, keepdims=True)) for h in heads]
    w_old = [jnp.exp(b_last[h] + m_old[h] - m_new[h]) for h in heads]
    kw = [kh[h] * jnp.exp(d_state[h] - m_new[h]) for h in heads]
    kv = [_dot_tn(kw[h], vh[h]) for h in heads]
    outs = []
    for h in heads:
        num = w_inter[h] * q_state[h] + _dot(s[h], vh[h])
        qn = jnp.sum(qh[h] * n_old[h], axis=-1, keepdims=True)
        den = w_inter[h] * qn + jnp.sum(s[h], axis=-1, keepdims=True)
        outs.append(num / jnp.maximum(jnp.abs(den), jnp.exp(-m_t[h])))
    for h in heads:
        c_ref[h // HEADS_D, h % HEADS_D] = w_old[h] * c_old[h] + kv[h]
    for d in range(2):
        hd = range(d * HEADS_D, (d + 1) * HEADS_D)
        o_refs[d][...] = jnp.concatenate([outs[h] for h in hd], axis=1)
        n_ref[d] = jnp.concatenate([w_old[h] * n_old[h] + jnp.sum(kw[h], axis=0, keepdims=True) for h in hd], axis=0)
        m_ref[d] = jnp.concatenate([jnp.broadcast_to(m_new[h], (1, m_ref.shape[2])) for h in hd], axis=0)

    @pl.when(c == n_chunks - 1)
    def _():
        cfin_ref[0] = c_ref[...]
        nfin_ref[0] = n_ref[...]
        mfin_ref[0] = m_ref[...]


def _mlstm(p, gcol, grow, states, n_seq, t_seq, row0):
    n_chunks = t_seq // CHUNK
    chunk0 = row0 // CHUNK

    in_specs, args, out_specs = [], [], []
    for d in range(2):
        def rows(b, c, d=d):
            return chunk0 + b * n_chunks + _chunk_index(d, c, n_chunks)

        def gidx(b, c, d=d, rows=rows):
            return (d, rows(b, c), 0, 0)

        in_specs += [pl.BlockSpec((CHUNK, D), lambda b, c, rows=rows: (rows(b, c), 0)),
                     pl.BlockSpec((CHUNK, D), lambda b, c, rows=rows: (rows(b, c), 1)),
                     pl.BlockSpec((1, 1, CHUNK, 16), gidx),
                     pl.BlockSpec((1, 1, 16, CHUNK), gidx)]
        args += [p, p, gcol, grow]
        out_specs.append(pl.BlockSpec((CHUNK, D), lambda b, c, d=d: (b * n_chunks + _chunk_index(d, c, n_chunks), 0)))
    c_spec = pl.BlockSpec((1, 2, HEADS_D, DK_D, DV_D), lambda b, c: (b, 0, 0, 0, 0))
    n_spec = pl.BlockSpec((1, 2, HEADS_D, DK_D), lambda b, c: (b, 0, 0, 0))
    m_spec = pl.BlockSpec((1, 2, HEADS_D, 128), lambda b, c: (b, 0, 0, 0))
    if states is not None:
        in_specs += [c_spec, n_spec, m_spec]
        args += list(states)
    m = n_seq * t_seq
    return pl.pallas_call(
        functools.partial(_mlstm_kernel, n_chunks=n_chunks, zero_init=states is None),
        out_shape=(jax.ShapeDtypeStruct((m, D), F32), jax.ShapeDtypeStruct((m, D), F32),
                   jax.ShapeDtypeStruct((n_seq, 2, HEADS_D, DK_D, DV_D), F32),
                   jax.ShapeDtypeStruct((n_seq, 2, HEADS_D, DK_D), F32),
                   jax.ShapeDtypeStruct((n_seq, 2, HEADS_D, 128), F32)),
        grid=(n_seq, n_chunks),
        in_specs=in_specs,
        out_specs=tuple(out_specs) + (c_spec, n_spec, m_spec),
        scratch_shapes=[pltpu.VMEM((2, HEADS_D, DK_D, DV_D), F32),
                        pltpu.VMEM((2, HEADS_D, DK_D), F32),
                        pltpu.VMEM((2, HEADS_D, 128), F32)],
        compiler_params=_cparams(2),
        name="mlstm",
    )(*args)


def _router_kernel(x_ref, sc_ref, sh_ref, wt_ref, b_ref, xs_ref, pos_ref, gate_ref, cnt_ref):
    h = (x_ref[...] * (1.0 + sc_ref[0]) + sh_ref[0]).astype(BF16)
    logits = _dot_nt(wt_ref[...], h) + b_ref[...]
    sub = lax.broadcasted_iota(jnp.int32, logits.shape, 0)
    onehots, vals = [], []
    for k in range(TOP_K):
        m = jnp.max(logits, axis=0, keepdims=True)
        idx = jnp.min(jnp.where(logits == m, sub, N_EXPERTS), axis=0, keepdims=True)
        hit = sub == idx
        logits = jnp.where(hit, -jnp.inf, logits)
        onehots.append(jnp.where(hit, 1.0, 0.0))
        vals.append(m)
    exps = [jnp.exp(v - vals[0]) for v in vals]
    den = exps[0] + exps[1] + exps[2] + exps[3]
    counts = [jnp.sum(o, axis=1, keepdims=True) for o in onehots]
    n = counts[0] + counts[1] + counts[2] + counts[3]
    n_al = jnp.floor((n + (SEG_ALIGN - 1)) * (1.0 / SEG_ALIGN)) * SEG_ALIGN
    er = lax.broadcasted_iota(jnp.int32, (N_EXPERTS, N_EXPERTS), 0)
    ec = lax.broadcasted_iota(jnp.int32, (N_EXPERTS, N_EXPERTS), 1)
    before_e = jnp.where(ec < er, 1.0, 0.0).astype(BF16)
    start = jnp.dot(before_e, jnp.broadcast_to(n_al, logits.shape).astype(BF16), preferred_element_type=F32)
    tr = lax.broadcasted_iota(jnp.int32, (TILE_TOK, TILE_TOK), 0)
    tc = lax.broadcasted_iota(jnp.int32, (TILE_TOK, TILE_TOK), 1)
    before_t = jnp.where(tr < tc, 1.0, 0.0).astype(BF16)
    rows = lax.broadcasted_iota(jnp.int32, (TILE_ROWS, TILE_TOK), 0)
    perm = jnp.zeros((TILE_ROWS, TILE_TOK), F32)
    prev = jnp.zeros_like(n)
    for k in range(TOP_K):
        cum = jnp.dot(onehots[k].astype(BF16), before_t, preferred_element_type=F32)
        pos = jnp.sum(onehots[k] * (start + prev + cum), axis=0, keepdims=True).astype(jnp.int32)
        prev = prev + counts[k]
        pos_ref[0, k:k + 1, :] = pos
        gate_ref[0, k:k + 1, :] = exps[k] / den
        perm = perm + jnp.where(rows == pos, 1.0, 0.0)
    xs_ref[...] = jnp.dot(perm.astype(BF16), h, preferred_element_type=F32)
    cnt_ref[0] = jnp.broadcast_to(n_al, cnt_ref.shape[1:]).astype(jnp.int32)


def _router(x, mod, w, b, n_prompt, t_lat):
    m = x.shape[0]
    nt = m // TILE_TOK
    cidx = functools.partial(_cond_index, tm=TILE_TOK, n_prompt=n_prompt, t_lat=t_lat)
    sel = pl.BlockSpec((1, TOP_K, TILE_TOK), lambda i: (i, 0, 0))
    return pl.pallas_call(
        _router_kernel,
        out_shape=(jax.ShapeDtypeStruct((nt * TILE_ROWS, D), F32),
                   jax.ShapeDtypeStruct((nt, TOP_K, TILE_TOK), jnp.int32),
                   jax.ShapeDtypeStruct((nt, TOP_K, TILE_TOK), F32),
                   jax.ShapeDtypeStruct((nt, N_EXPERTS, 128), jnp.int32)),
        grid=(nt,),
        in_specs=[pl.BlockSpec((TILE_TOK, D), lambda i: (i, 0)),
                  pl.BlockSpec((1, 1, D), lambda i: (cidx(i), 0, 4)),
                  pl.BlockSpec((1, 1, D), lambda i: (cidx(i), 0, 3)),
                  pl.BlockSpec((N_EXPERTS, D), lambda i: (0, 0)),
                  pl.BlockSpec((N_EXPERTS, 1), lambda i: (0, 0))],
        out_specs=(pl.BlockSpec((TILE_ROWS, D), lambda i: (i, 0)), sel, sel,
                   pl.BlockSpec((1, N_EXPERTS, 128), lambda i: (i, 0, 0))),
        compiler_params=_cparams(1),
        name="router",
    )(x, mod, mod, w.T, b.reshape(N_EXPERTS, 1))


def _expert_kernel(be_ref, boff_ref, brows_ref, tlo_ref, thi_ref, nb_ref, scum_ref, sn_ref, ssrc_ref, tused_ref,
                   xs_hbm, win_ref, bin_ref, wout_ref, bout_ref, ys_hbm,
                   xbuf, ybuf, sem_in, sem_out, winbf, woutbf, *, n_tiles):
    i = pl.program_id(0)
    nb = nb_ref[0]
    slot = i % 2

    def for_pieces(b, fn):
        e = be_ref[b]
        off = boff_ref[b]

        def body(t, carry):
            j = e * n_tiles + t
            s0 = scum_ref[j]
            lo = jnp.maximum(s0, off)
            hi = jnp.minimum(s0 + sn_ref[j], off + EXPERT_ROWS)

            @pl.when(hi > lo)
            def _():
                fn(pl.multiple_of(ssrc_ref[j] + (lo - s0), SEG_ALIGN), pl.multiple_of(lo - off, SEG_ALIGN),
                   pl.multiple_of(hi - lo, SEG_ALIGN))
            return carry
        lax.fori_loop(tlo_ref[b], thi_ref[b], body, 0)

    def in_copy(s, src, dst, n):
        return pltpu.make_async_copy(xs_hbm.at[pl.ds(src, n), :], xbuf.at[s, pl.ds(dst, n), :], sem_in.at[s])

    def out_copy(s, src, dst, n):
        return pltpu.make_async_copy(ybuf.at[s, pl.ds(dst, n), :], ys_hbm.at[pl.ds(src, n), :], sem_out.at[s])

    def wait_block(copy, b, s):
        n = pl.multiple_of(brows_ref[b], SEG_ALIGN)

        @pl.when(n > 0)
        def _():
            copy(s, 0, 0, n).wait()

    def tail_copy(t):
        used = pl.multiple_of(tused_ref[t], SEG_ALIGN)
        n = pl.multiple_of(TILE_ROWS - used, SEG_ALIGN)
        return n, pltpu.make_async_copy(ybuf.at[1, pl.ds(0, n), :],
                                        ys_hbm.at[pl.ds(pl.multiple_of(t * TILE_ROWS + used, SEG_ALIGN), n), :],
                                        sem_out.at[1])

    @pl.when(i == 0)
    def _():
        xbuf[...] = jnp.zeros(xbuf.shape, F32)
        ybuf[1] = jnp.zeros(ybuf.shape[1:], F32)
        for_pieces(0, lambda src, dst, n: in_copy(0, src, dst, n).start())

        def start_tail(t, carry):
            n, copy = tail_copy(t)

            @pl.when(n > 0)
            def _():
                copy.start()
            return carry

        def wait_tail(t, carry):
            n, copy = tail_copy(t)

            @pl.when(n > 0)
            def _():
                copy.wait()
            return carry

        lax.fori_loop(0, n_tiles, start_tail, 0)
        lax.fori_loop(0, n_tiles, wait_tail, 0)

    @pl.when(i + 1 < nb)
    def _():
        for_pieces(i + 1, lambda src, dst, n: in_copy(1 - slot, src, dst, n).start())

    new_expert = jnp.logical_or(i == 0, be_ref[i] != be_ref[jnp.maximum(i - 1, 0)])

    @pl.when(jnp.logical_and(new_expert, i < nb))
    def _():
        winbf[...] = win_ref[0].astype(BF16)
        woutbf[...] = wout_ref[0].astype(BF16)

    @pl.when(i < nb)
    def _():
        wait_block(in_copy, i, slot)
        x = xbuf[slot].astype(BF16)
        hmid = jnp.dot(x, winbf[...], preferred_element_type=F32) + bin_ref[0]
        glu = jnp.minimum(hmid[:, :D_FF], SWIGLU_LIMIT)
        lin = jnp.clip(hmid[:, D_FF:], -SWIGLU_LIMIT, SWIGLU_LIMIT)
        act = glu * _sigmoid(SWIGLU_ALPHA * glu) * (lin + 1.0)
        y = jnp.dot(act.astype(BF16), woutbf[...], preferred_element_type=F32) + bout_ref[0]

        @pl.when(i >= 1)
        def _():
            wait_block(out_copy, i - 1, 1 - slot)

        ybuf[slot] = y
        for_pieces(i, lambda src, dst, n: out_copy(slot, src, dst, n).start())

        @pl.when(i == nb - 1)
        def _():
            wait_block(out_copy, i, slot)


def _experts(xs, tables, layer, w_in, b_in, w_out, b_out, n_tiles):
    n_blocks = tables[0].shape[0]
    depth = w_in.shape[0]

    def wmap(i, be, *_):
        return (layer, be[i], 0, 0)

    grid_spec = pltpu.PrefetchScalarGridSpec(
        num_scalar_prefetch=len(tables),
        grid=(n_blocks,),
        in_specs=[pl.BlockSpec(memory_space=pl.ANY),
                  pl.BlockSpec((None, 1, D, 2 * D_FF), wmap),
                  pl.BlockSpec((None, 1, 1, 2 * D_FF), wmap),
                  pl.BlockSpec((None, 1, D_FF, D), wmap),
                  pl.BlockSpec((None, 1, 1, D), wmap)],
        out_specs=pl.BlockSpec(memory_space=pl.ANY),
        scratch_shapes=[pltpu.VMEM((2, EXPERT_ROWS, D), F32),
                        pltpu.VMEM((2, EXPERT_ROWS, D), F32),
                        pltpu.SemaphoreType.DMA((2,)),
                        pltpu.SemaphoreType.DMA((2,)),
                        pltpu.VMEM((D, 2 * D_FF), BF16),
                        pltpu.VMEM((D_FF, D), BF16)])
    return pl.pallas_call(
        functools.partial(_expert_kernel, n_tiles=n_tiles),
        out_shape=jax.ShapeDtypeStruct(xs.shape, F32),
        grid_spec=grid_spec,
        compiler_params=_cparams(1),
        name="experts",
    )(*tables, xs, w_in, b_in.reshape(depth, N_EXPERTS, 1, 2 * D_FF), w_out, b_out.reshape(depth, N_EXPERTS, 1, D))


def _combine_kernel(ys_ref, pos_ref, gate_ref, x_ref, g_ref, lg_ref, lb_ref, o_ref):
    rows = lax.broadcasted_iota(jnp.int32, (TILE_ROWS, TILE_TOK), 0)
    perm = jnp.zeros((TILE_ROWS, TILE_TOK), F32)
    gmat = jnp.zeros((TILE_ROWS, TILE_TOK), F32)
    for k in range(TOP_K):
        hit = rows == pos_ref[0, k:k + 1, :]
        perm = perm + jnp.where(hit, 1.0, 0.0)
        gmat = gmat + jnp.where(hit, gate_ref[0, k:k + 1, :], 0.0)
    row_gate = jnp.sum(gmat, axis=1, keepdims=True)
    z = ys_ref[...] * row_gate
    hi, lo = _split2(z)
    p = perm.astype(BF16)
    dims = (((0,), (0,)), ((), ()))
    y = (lax.dot_general(p, hi, dims, preferred_element_type=F32)
         + lax.dot_general(p, lo, dims, preferred_element_type=F32))
    o_ref[...] = _layer_norm(_ALPHA * x_ref[...] + g_ref[0] * y, lg_ref[...], lb_ref[...])


def _combine(ys, pos, gates, x, mod, ln_g, ln_b, n_prompt, t_lat):
    m = x.shape[0]
    cidx = functools.partial(_cond_index, tm=TILE_TOK, n_prompt=n_prompt, t_lat=t_lat)
    row = pl.BlockSpec((TILE_TOK, D), lambda i: (i, 0))
    vec = pl.BlockSpec((1, D), lambda i: (0, 0))
    sel = pl.BlockSpec((1, TOP_K, TILE_TOK), lambda i: (i, 0, 0))
    return pl.pallas_call(
        _combine_kernel,
        out_shape=jax.ShapeDtypeStruct((m, D), F32),
        grid=(m // TILE_TOK,),
        in_specs=[pl.BlockSpec((TILE_ROWS, D), lambda i: (i, 0)), sel, sel, row,
                  pl.BlockSpec((1, 1, D), lambda i: (cidx(i), 0, 5)), vec, vec],
        out_specs=row,
        compiler_params=_cparams(1),
        name="combine",
    )(ys, pos, gates, x, mod, ln_g.reshape(1, D), ln_b.reshape(1, D))


def _routing_tables(seg_n):
    n_tiles = seg_n.shape[0]
    seg_src = (jnp.arange(n_tiles, dtype=jnp.int32)[:, None] * TILE_ROWS
               + jnp.cumsum(seg_n, axis=1) - seg_n)
    seg_cum = jnp.cumsum(seg_n, axis=0) - seg_n
    count = jnp.sum(seg_n, axis=0)
    padded = (count + EXPERT_ROWS - 1) // EXPERT_ROWS * EXPERT_ROWS
    pad_end = jnp.cumsum(padded)
    pad_start = pad_end - padded
    n_blocks = n_tiles * TILE_ROWS // EXPERT_ROWS + N_EXPERTS
    n_used = pad_end[-1] // EXPERT_ROWS
    blocks = jnp.arange(n_blocks, dtype=jnp.int32)
    owner = jnp.minimum(jnp.sum(blocks[:, None] * EXPERT_ROWS >= pad_end[None, :], axis=1), N_EXPERTS - 1)
    last_owner = jnp.minimum(jnp.sum((n_used - 1) * EXPERT_ROWS >= pad_end), N_EXPERTS - 1)
    block_expert = jnp.where(blocks < n_used, owner, last_owner).astype(jnp.int32)
    own = block_expert[:, None] == jnp.arange(N_EXPERTS, dtype=jnp.int32)[None, :]
    block_off = blocks * EXPERT_ROWS - jnp.sum(jnp.where(own, pad_start[None, :], 0), axis=1)
    block_rows = jnp.clip(jnp.sum(jnp.where(own, count[None, :], 0), axis=1) - block_off, 0, EXPERT_ROWS)
    cum_b = jnp.sum(jnp.where(own[:, None, :], seg_cum[None], 0), axis=2)
    n_b = jnp.sum(jnp.where(own[:, None, :], seg_n[None], 0), axis=2)
    t_lo = jnp.sum(cum_b + n_b <= block_off[:, None], axis=1)
    t_hi = jnp.sum(cum_b < block_off[:, None] + EXPERT_ROWS, axis=1)
    i32 = lambda a: a.astype(jnp.int32)
    return (block_expert, i32(block_off), i32(block_rows), i32(t_lo), i32(t_hi), i32(n_used).reshape(1),
            i32(seg_cum.T.reshape(-1)), i32(seg_n.T.reshape(-1)), i32(seg_src.T.reshape(-1)),
            i32(jnp.sum(seg_n, axis=1)))


def _moe(x, mod, layer, w_router, b_router, w_in, b_in, w_out, b_out, ln_g, ln_b, n_prompt, t_lat):
    xs, pos, gates, cnt = _router(x, mod, w_router, b_router, n_prompt, t_lat)
    tables = _routing_tables(cnt[:, :, 0])
    ys = _experts(xs, tables, layer, w_in, b_in, w_out, b_out, cnt.shape[0])
    return _combine(ys, pos, gates, x, mod, ln_g, ln_b, n_prompt, t_lat)


def kernel(x_prompt, x_sample, cache_k_a, cache_v_a, state_b, cache_k_c, cache_v_c, state_d_c, state_d_n, state_d_m, c, c_ctx, w_mod, b_mod, ln_g, ln_b, w_in_a, q_norm_a, k_norm_a, w_out_a, w_in_b, w_conv_b, w_gate_b, a_log_b, dt_bias_b, o_norm_b, w_out_b, w_in_c, rpb_c, w_out_c, w_in_d, w_gate_d, b_gate_d, o_norm_d, w_out_d, w_router, b_router, w_exp_in, b_exp_in, w_exp_out, b_exp_out):
    bp, t_p, _ = x_prompt.shape
    bs, t_s, _ = x_sample.shape
    past = cache_k_a.shape[2]
    depth = w_mod.shape[0]
    assert depth == 4 and bs + 1 <= N_COND
    n_p, n_s = bp * t_p, bs * t_s
    x = jnp.concatenate([x_prompt.reshape(n_p, D), x_sample.reshape(n_s, D)], axis=0)
    cond = jnp.concatenate([c_ctx.reshape(1, D), c, jnp.zeros((N_COND - 1 - bs, D), F32)], axis=0)
    res = functools.partial(_proj_res_ln, n_prompt=n_p, t_lat=t_s)
    outs = {}
    for i in range(depth):
        kind, j = i % 4, i // 4
        mod = _modulation_table(cond, w_mod, b_mod[i], i).reshape(N_COND, 1, 6 * D)
        proj = functools.partial(_modmm, x, mod, 1, 0, n_prompt=n_p, t_lat=t_s)
        if kind == 0:
            p = proj(w_in_a[j])
            q, k = _prep_a(p, q_norm_a[j], k_norm_a[j], n_p, t_s)
            hq, hk = HEADS_A * DH_A, KV_A * DH_A
            vcol = hq // hk + 1
            o_p = _attention(q, pl.BlockSpec((t_p, hq), lambda b: (b, 0)),
                             [(k, pl.BlockSpec((t_p, hk), lambda b: (b, 0)),
                               p, pl.BlockSpec((t_p, hk), lambda b: (b, vcol)))],
                             n_p, (bp,), HEADS_A, KV_A, DH_A, pl.BlockSpec((t_p, hq), lambda b: (b, 0)))
            tq = 256
            qb0, lat0 = n_p // tq, n_p // t_s
            kc = cache_k_a[:, j].reshape(bs * past, hk)
            vc = cache_v_a[:, j].reshape(bs * past, hk)
            o_s = _attention(q, pl.BlockSpec((tq, hq), lambda b, t: (qb0 + b * (t_s // tq) + t, 0)),
                             [(kc, pl.BlockSpec((past, hk), lambda b, t: (b, 0)),
                               vc, pl.BlockSpec((past, hk), lambda b, t: (b, 0))),
                              (k, pl.BlockSpec((t_s, hk), lambda b, t: (lat0 + b, 0)),
                               p, pl.BlockSpec((t_s, hk), lambda b, t: (lat0 + b, vcol)))],
                             n_s, (bs, t_s // tq), HEADS_A, KV_A, DH_A,
                             pl.BlockSpec((tq, hq), lambda b, t: (b * (t_s // tq) + t, 0)))
            outs["k_a"] = k[:n_p].reshape(bp, 1, t_p, KV_A, DH_A)
            outs["v_a"] = p[:n_p, hq + hk:].reshape(bp, 1, t_p, KV_A, DH_A)
            x = res("plain", (o_p, o_s), w_out_a[j], x, mod, 2, ln_g[i, 0], ln_b[i, 0])
        elif kind == 1:
            p = proj(w_in_b[j])
            raw = proj(w_gate_b[j])
            zeros8 = jnp.zeros((2, HEADS_B), F32)
            a32 = jnp.concatenate([a_log_b[j], zeros8], axis=1).reshape(1, 32)
            dt32 = jnp.concatenate([dt_bias_b[j], zeros8], axis=1).reshape(1, 32)
            gcol, grow = _gate_layouts(_gates("gdn", raw, a32, dt32))
            qp, kp, vp = _prep_b(p, w_conv_b[j], 0, n_p, t_p)
            qs, ks, vs = _prep_b(p, w_conv_b[j], n_p, n_s, t_s)
            o_pf, o_pb, s_p = _gdn(qp, kp, vp, gcol, grow, None, bp, t_p, 0)
            o_sf, o_sb, _ = _gdn(qs, ks, vs, gcol, grow, state_b[:, j], bs, t_s, n_p // CHUNK)
            outs["s_b"] = s_p.reshape(bp, 1, 2, HEADS_B, DK_B, DK_B)
            x = res("silu", (o_pf, o_pb, o_sf, o_sb, p, o_norm_b[j].reshape(1, DK_B)), w_out_b[j], x, mod, 2, ln_g[i, 0], ln_b[i, 0],
                    z_colblock=3)
        elif kind == 2:
            hd = HEADS_C * DH_C
            pq, pk, pv = (proj(w_in_c[j][:, s * hd:(s + 1) * hd]) for s in range(3))
            blk = pl.BlockSpec((t_p, hd), lambda b: (b, 0))
            o_p = _attention(pq, blk, [(pk, blk, pv, blk)], n_p, (bp,), HEADS_C, HEADS_C, DH_C, blk)
            o_s = _nbr_attention(pq, pk, pv, cache_k_c[:, j].reshape(bs * past, hd),
                                 cache_v_c[:, j].reshape(bs * past, hd), rpb_c[j], n_p, bs, t_s, past)
            outs["k_c"] = pk[:n_p].reshape(bp, 1, t_p, HEADS_C, DH_C)
            outs["v_c"] = pv[:n_p].reshape(bp, 1, t_p, HEADS_C, DH_C)
            x = res("plain", (o_p, o_s), w_out_c[j], x, mod, 2, ln_g[i, 0], ln_b[i, 0])
        else:
            p = proj(w_in_d[j])
            raw = proj(w_gate_d[j])
            gcol, grow = _gate_layouts(_gates("mlstm", raw, b_gate_d[j].reshape(1, 32), jnp.zeros((1, 32), F32)))
            h_pf, h_pb, c_p, n_fin, m_fin = _mlstm(p, gcol, grow, None, bp, t_p, 0)
            m0 = jnp.broadcast_to(state_d_m[:, j][..., None], (bs, 2, HEADS_D, 128))
            h_sf, h_sb, _, _, _ = _mlstm(p, gcol, grow, (state_d_c[:, j], state_d_n[:, j], m0), bs, t_s, n_p)
            outs["c_d"] = c_p.reshape(bp, 1, 2, HEADS_D, DK_D, DV_D)
            outs["n_d"] = n_fin.reshape(bp, 1, 2, HEADS_D, DK_D)
            outs["m_d"] = m_fin[..., 0].reshape(bp, 1, 2, HEADS_D)
            x = res("sigmoid", (h_pf, h_pb, h_sf, h_sb, p, o_norm_d[j].reshape(1, DV_D)), w_out_d[j], x, mod, 2, ln_g[i, 0], ln_b[i, 0],
                    z_colblock=2)
        x = _moe(x, mod, i, w_router[i], b_router[i], w_exp_in, b_exp_in, w_exp_out, b_exp_out,
                 ln_g[i, 1], ln_b[i, 1], n_p, t_s)
    return (x[:n_p].reshape(bp, t_p, D), x[n_p:].reshape(bs, t_s, D), outs["k_a"], outs["v_a"], outs["s_b"],
            outs["k_c"], outs["v_c"], outs["c_d"], outs["n_d"], outs["m_d"])
```

```python
import functools

import jax
import jax.numpy as jnp
import numpy as np
from jax import lax
from jax.experimental import pallas as pl
from jax.experimental.pallas import tpu as pltpu

F32 = jnp.float32
BF16 = jnp.bfloat16

D = 1024
GRID_W = 64
HEADS_A, KV_A, DH_A = 16, 4, 64
ROPE_THETA = 10000.0
HEADS_B, DK_B = 8, 128
CONV_W = 3
CHUNK = 64
HEADS_C, DH_C = 16, 64
WIN_R, WIN_C = 8, 16
HEADS_D, DK_D, DV_D = 8, 64, 128
N_EXPERTS, TOP_K, D_FF = 32, 4, 1024
SWIGLU_LIMIT, SWIGLU_ALPHA = 7.0, 1.702
NEG_INF = -1e30
N_COND = 8
EXPERT_ROWS = 512
TILE_TOK = 256
SEG_ALIGN = 8
TILE_ROWS = (TILE_TOK * TOP_K + N_EXPERTS * (SEG_ALIGN - 1) + 7) // 8 * 8
VMEM_LIMIT = 56 * 1024 * 1024
_ALPHA = float((2 * 4) ** 0.25)


def _cparams(n_axes):
    return pltpu.CompilerParams(dimension_semantics=("arbitrary",) * n_axes, vmem_limit_bytes=VMEM_LIMIT)


def _dot(a, b):
    return jnp.dot(a.astype(BF16), b.astype(BF16), preferred_element_type=F32)


def _dot_nt(a, b):
    return lax.dot_general(a.astype(BF16), b.astype(BF16), (((1,), (1,)), ((), ())), preferred_element_type=F32)


def _dot_tn(a, b):
    return lax.dot_general(a.astype(BF16), b.astype(BF16), (((0,), (0,)), ((), ())), preferred_element_type=F32)


def _split2(x):
    hi = x.astype(BF16)
    lo = (x - hi.astype(F32)).astype(BF16)
    return hi, lo


def _split3(x):
    hi = x.astype(BF16)
    r = x - hi.astype(F32)
    mid = r.astype(BF16)
    lo = (r - mid.astype(F32)).astype(BF16)
    return hi, mid, lo


def _dot_hi(a, b):
    ah, al = _split2(a)
    bh, bl = _split2(b)
    d = functools.partial(jnp.dot, preferred_element_type=F32)
    return d(ah, bh) + (d(ah, bl) + d(al, bh))


def _mask_dot(mask_bf16, x):
    d = functools.partial(jnp.dot, preferred_element_type=F32)
    hi, mid, lo = _split3(x)
    return d(mask_bf16, hi) + (d(mask_bf16, mid) + d(mask_bf16, lo))


def _dot_mask(x, mask_bf16):
    d = functools.partial(jnp.dot, preferred_element_type=F32)
    hi, mid, lo = _split3(x)
    return d(hi, mask_bf16) + (d(mid, mask_bf16) + d(lo, mask_bf16))


def _layer_norm(x, g, b):
    mu = jnp.mean(x, axis=-1, keepdims=True)
    xc = x - mu
    var = jnp.mean(xc * xc, axis=-1, keepdims=True)
    return xc * lax.rsqrt(var + 1e-5) * g + b


def _sigmoid(x):
    return 1.0 / (1.0 + jnp.exp(-x))


def _cond_index(i, tm, n_prompt, t_lat):
    npt = n_prompt // tm
    return jnp.where(i < npt, 0, 1 + (i - npt) // (t_lat // tm))


def _mod_kernel(c_ref, w_ref, b_ref, o_ref):
    c = c_ref[...]
    o_ref[...] = _dot(c * _sigmoid(c), w_ref[...]) + b_ref[...]


def _modulation_table(cond, w, b, layer):
    n = w.shape[2]
    tn = 1024
    return pl.pallas_call(
        _mod_kernel,
        out_shape=jax.ShapeDtypeStruct((N_COND, n), F32),
        grid=(n // tn,),
        in_specs=[pl.BlockSpec((N_COND, D), lambda j: (0, 0)),
                  pl.BlockSpec((None, D, tn), lambda j: (layer, 0, j)),
                  pl.BlockSpec((1, tn), lambda j: (0, j))],
        out_specs=pl.BlockSpec((N_COND, tn), lambda j: (0, j)),
        compiler_params=_cparams(1),
        name="mod_table",
    )(cond, w, b.reshape(1, n))


def _modmm_kernel(x_ref, sc_ref, sh_ref, w_ref, o_ref, wbf_ref):
    @pl.when(pl.program_id(1) == 0)
    def _():
        wbf_ref[...] = w_ref[...].astype(BF16)

    h = x_ref[...] * (1.0 + sc_ref[0]) + sh_ref[0]
    o_ref[...] = jnp.dot(h.astype(BF16), wbf_ref[...], preferred_element_type=F32)


def _col_tile(n):
    for t in (2048, 1536, 1024, 512, 256, 128):
        if n % t == 0:
            return t
    return n


def _modmm(x, mod, which_sc, which_sh, w, n_prompt, t_lat, tm=1024):
    m = x.shape[0]
    n = w.shape[1]
    tn = _col_tile(n)
    cidx = functools.partial(_cond_index, tm=tm, n_prompt=n_prompt, t_lat=t_lat)
    return pl.pallas_call(
        _modmm_kernel,
        out_shape=jax.ShapeDtypeStruct((m, n), F32),
        grid=(n // tn, m // tm),
        in_specs=[pl.BlockSpec((tm, D), lambda j, i: (i, 0)),
                  pl.BlockSpec((1, 1, D), lambda j, i: (cidx(i), 0, which_sc)),
                  pl.BlockSpec((1, 1, D), lambda j, i: (cidx(i), 0, which_sh)),
                  pl.BlockSpec((D, tn), lambda j, i: (0, j))],
        out_specs=pl.BlockSpec((tm, tn), lambda j, i: (i, j)),
        scratch_shapes=[pltpu.VMEM((D, tn), BF16)],
        compiler_params=_cparams(2),
        name="mod_proj",
    )(x, mod, mod, w)


def _proj_kernel(*refs, kind, prompt_tiles):
    if kind == "plain":
        ap_ref, as_ref, w_ref, x_ref, g_ref, lg_ref, lb_ref, o_ref, wbf_ref = refs
    else:
        (pf_ref, pb_ref, sf_ref, sb_ref, z_ref, on_ref, w_ref, x_ref, g_ref, lg_ref, lb_ref,
         o_ref, wbf_ref, abuf_ref) = refs
    is_prompt = pl.program_id(0) < prompt_tiles

    @pl.when(pl.program_id(0) == 0)
    def _():
        wbf_ref[...] = w_ref[...].astype(BF16)

    if kind == "plain":
        a = jnp.where(is_prompt, ap_ref[...], as_ref[...]).astype(BF16)
    else:
        s = jnp.where(is_prompt, pf_ref[...] + pb_ref[...], sf_ref[...] + sb_ref[...])
        on = on_ref[...]
        hd = on.shape[-1]
        for h in range(D // hd):
            sl = slice(h * hd, (h + 1) * hd)
            sh = s[:, sl]
            y = sh * lax.rsqrt(jnp.mean(sh * sh, axis=-1, keepdims=True) + 1e-6) * on
            z = z_ref[:, sl]
            gate = _sigmoid(z)
            if kind == "silu":
                gate = z * gate
            abuf_ref[:, sl] = (y * gate).astype(BF16)
        a = abuf_ref[...]
    y = jnp.dot(a, wbf_ref[...], preferred_element_type=F32)
    o_ref[...] = _layer_norm(_ALPHA * x_ref[...] + g_ref[0] * y, lg_ref[...], lb_ref[...])


def _proj_res_ln(kind, a_args, w, x, mod, which_g, ln_g, ln_b, n_prompt, t_lat, z_colblock=0, tm=512):
    m = x.shape[0]
    cidx = functools.partial(_cond_index, tm=tm, n_prompt=n_prompt, t_lat=t_lat)
    row = pl.BlockSpec((tm, D), lambda i: (i, 0))
    vec = pl.BlockSpec((1, D), lambda i: (0, 0))
    npt = n_prompt // tm
    prompt_rows = pl.BlockSpec((tm, D), lambda i: (jnp.minimum(i, npt - 1), 0))
    latent_rows = pl.BlockSpec((tm, D), lambda i: (jnp.maximum(i - npt, 0), 0))
    if kind == "plain":
        a_specs = [prompt_rows, latent_rows]
        scratch = [pltpu.VMEM((D, D), BF16)]
    else:
        on = a_args[5]
        a_specs = [prompt_rows, prompt_rows, latent_rows, latent_rows,
                   pl.BlockSpec((tm, D), lambda i: (i, z_colblock)),
                   pl.BlockSpec((1, on.shape[-1]), lambda i: (0, 0))]
        scratch = [pltpu.VMEM((D, D), BF16), pltpu.VMEM((tm, D), BF16)]
    return pl.pallas_call(
        functools.partial(_proj_kernel, kind=kind, prompt_tiles=npt),
        out_shape=jax.ShapeDtypeStruct((m, D), F32),
        grid=(m // tm,),
        in_specs=a_specs + [pl.BlockSpec((D, D), lambda i: (0, 0)), row,
                            pl.BlockSpec((1, 1, D), lambda i: (cidx(i), 0, which_g)), vec, vec],
        out_specs=row,
        scratch_shapes=scratch,
        compiler_params=_cparams(1),
        name="out_proj_" + kind,
    )(*a_args, w, x, mod, ln_g.reshape(1, D), ln_b.reshape(1, D))


def _prep_a_kernel(p_ref, qn_ref, kn_ref, cos_ref, sin_ref, q_ref, k_ref):
    cos = cos_ref[...]
    sin = sin_ref[...]
    width = 2 * DH_A
    lane = lax.broadcasted_iota(jnp.int32, cos.shape, 1)
    first_half = lane % DH_A < DH_A // 2
    r = lax.broadcasted_iota(jnp.int32, (width, width), 0)
    c = lax.broadcasted_iota(jnp.int32, (width, width), 1)
    same_head = jnp.where(r // DH_A == c // DH_A, 1.0, 0.0).astype(BF16)

    def pair(x, g):
        ms = _dot_mask(x * x, same_head) * (1.0 / DH_A)
        y = x * lax.rsqrt(ms + 1e-6) * g
        swapped = jnp.where(first_half, pltpu.roll(y, width - DH_A // 2, axis=1), pltpu.roll(y, DH_A // 2, axis=1))
        return y * cos + swapped * sin

    for j in range(HEADS_A // 2):
        sl = slice(j * width, (j + 1) * width)
        q_ref[:, sl] = pair(p_ref[:, sl], qn_ref[...])
    for j in range(KV_A // 2):
        sl = slice(j * width, (j + 1) * width)
        k_ref[:, sl] = pair(p_ref[:, HEADS_A * DH_A + j * width: HEADS_A * DH_A + (j + 1) * width], kn_ref[...])


def _rope_tables(t_lat, tm):
    quarter = DH_A // 4
    t = np.arange(t_lat)
    inv_freq = ROPE_THETA ** (-np.arange(quarter, dtype=np.float32) / quarter)
    ang = np.concatenate([(t // GRID_W)[:, None] * inv_freq, (t % GRID_W)[:, None] * inv_freq], axis=-1)
    ang = jnp.asarray(ang, F32)
    cos, sin = jnp.cos(ang), jnp.sin(ang)
    cos64 = jnp.concatenate([cos, cos], axis=-1)
    sin64 = jnp.concatenate([-sin, sin], axis=-1)
    cos64 = jnp.concatenate([jnp.ones((tm, DH_A), F32), cos64], axis=0)
    sin64 = jnp.concatenate([jnp.zeros((tm, DH_A), F32), sin64], axis=0)
    return jnp.tile(cos64, (1, 2)), jnp.tile(sin64, (1, 2))


def _prep_a(p, q_norm, k_norm, n_prompt, t_lat, tm=256):
    m = p.shape[0]
    cos64, sin64 = _rope_tables(t_lat, tm)
    npt = n_prompt // tm

    def tab(i):
        return (jnp.where(i < npt, 0, 1 + (i - npt) % (t_lat // tm)), 0)

    return pl.pallas_call(
        _prep_a_kernel,
        out_shape=(jax.ShapeDtypeStruct((m, HEADS_A * DH_A), F32), jax.ShapeDtypeStruct((m, KV_A * DH_A), F32)),
        grid=(m // tm,),
        in_specs=[pl.BlockSpec((tm, p.shape[1]), lambda i: (i, 0)),
                  pl.BlockSpec((1, 2 * DH_A), lambda i: (0, 0)),
                  pl.BlockSpec((1, 2 * DH_A), lambda i: (0, 0)),
                  pl.BlockSpec((tm, 2 * DH_A), tab),
                  pl.BlockSpec((tm, 2 * DH_A), tab)],
        out_specs=(pl.BlockSpec((tm, HEADS_A * DH_A), lambda i: (i, 0)),
                   pl.BlockSpec((tm, KV_A * DH_A), lambda i: (i, 0))),
        compiler_params=_cparams(1),
        name="prep_a",
    )(p, jnp.tile(q_norm.reshape(1, DH_A), (1, 2)), jnp.tile(k_norm.reshape(1, DH_A), (1, 2)), cos64, sin64)


def _softmax_pv(scores, values):
    m = scores[0].max(axis=-1, keepdims=True)
    for s in scores[1:]:
        m = jnp.maximum(m, s.max(axis=-1, keepdims=True))
    ps = [jnp.exp(s - m) for s in scores]
    den = ps[0].sum(axis=-1, keepdims=True)
    for p in ps[1:]:
        den = den + p.sum(axis=-1, keepdims=True)
    o = _dot(ps[0], values[0])
    for p, v in zip(ps[1:], values[1:]):
        o = o + _dot(p, v)
    return o * (1.0 / den)


def _pipelined_heads(n_heads, score_fn, finish_fn, batch=8):
    pending = []
    for h0 in range(0, n_heads, batch):
        current = [(h, score_fn(h)) for h in range(h0, min(h0 + batch, n_heads))]
        for h, sc in pending:
            finish_fn(h, sc)
        pending = current
    for h, sc in pending:
        finish_fn(h, sc)


def _attn_kernel(*refs, heads, kv_heads, dh, nseg):
    q_ref = refs[0]
    kv = refs[1:1 + 2 * nseg]
    o_ref = refs[1 + 2 * nseg]
    scale = dh ** -0.5
    group = heads // kv_heads

    def scores(h):
        ksl = slice(h // group * dh, (h // group + 1) * dh)
        q = q_ref[:, h * dh:(h + 1) * dh] * scale
        return [_dot_nt(q, kv[2 * s][:, ksl]) for s in range(nseg)]

    def finish(h, sc):
        ksl = slice(h // group * dh, (h // group + 1) * dh)
        o_ref[:, h * dh:(h + 1) * dh] = _softmax_pv(sc, [kv[2 * s + 1][:, ksl] for s in range(nseg)])

    _pipelined_heads(heads, scores, finish)


def _attention(q_arr, q_spec, segs, out_rows, grid, heads, kv_heads, dh, out_spec):
    in_specs = [q_spec]
    args = [q_arr]
    for (k_arr, k_spec, v_arr, v_spec) in segs:
        in_specs += [k_spec, v_spec]
        args += [k_arr, v_arr]
    return pl.pallas_call(
        functools.partial(_attn_kernel, heads=heads, kv_heads=kv_heads, dh=dh, nseg=len(segs)),
        out_shape=jax.ShapeDtypeStruct((out_rows, heads * dh), F32),
        grid=grid,
        in_specs=in_specs,
        out_specs=out_spec,
        compiler_params=_cparams(len(grid)),
        name="attention",
    )(*args)


def _window_start(r, n_rows):
    wr = min(WIN_R, n_rows)
    return jnp.minimum(jnp.maximum(r - wr // 2, 0), n_rows - wr)


def _nbr_kernel(q_ref, kl_ref, vl_ref, kc_ref, vc_ref, bias_ref, o_ref, *, n_rows):
    r = pl.program_id(1)
    wr = min(WIN_R, n_rows)
    start = pl.multiple_of(_window_start(r, n_rows) * GRID_W, GRID_W)
    scale = DH_C ** -0.5
    window = pl.ds(start, wr * GRID_W)

    def scores(h):
        sl = slice(h * DH_C, (h + 1) * DH_C)
        q = q_ref[:, sl] * scale
        return [_dot_nt(q, kl_ref[window, sl]) + bias_ref[0, h], _dot_nt(q, kc_ref[:, sl])]

    def finish(h, sc):
        sl = slice(h * DH_C, (h + 1) * DH_C)
        o_ref[:, sl] = _softmax_pv(sc, [vl_ref[window, sl], vc_ref[:, sl]])

    _pipelined_heads(HEADS_C, scores, finish)


def _nbr_bias(rpb, n_rows):
    wr = min(WIN_R, n_rows)
    cols = np.arange(GRID_W)
    col_start = np.clip(cols - WIN_C // 2, 0, GRID_W - WIN_C)
    col_ok = (cols[None, :] >= col_start[:, None]) & (cols[None, :] < col_start[:, None] + WIN_C)
    dc = np.clip(cols[None, :] - cols[:, None], -(WIN_C - 1), WIN_C - 1) + (WIN_C - 1)
    n_off = 2 * WIN_R - wr
    pick = np.zeros((2 * WIN_C - 1, GRID_W * GRID_W), np.float32)
    pick[dc.reshape(-1), np.arange(GRID_W * GRID_W)] = 1.0
    tab = jnp.einsum("hrd,dx->hrx", rpb.astype(F32), jnp.asarray(pick), precision=lax.Precision.HIGHEST)
    tab = jnp.where(col_ok[None, None], tab.reshape(HEADS_C, 2 * WIN_R - 1, GRID_W, GRID_W), NEG_INF)
    b = jnp.stack([tab[:, off:off + wr] for off in range(n_off)], axis=0)
    return jnp.transpose(b, (0, 1, 3, 2, 4)).reshape(n_off, HEADS_C, GRID_W, wr * GRID_W)


def _nbr_attention(q, k, v, k_ctx, v_ctx, rpb, n_prompt, n_lat_b, t_lat, past):
    n_rows = t_lat // GRID_W
    wr = min(WIN_R, n_rows)
    bias = _nbr_bias(rpb, n_rows)
    row0 = n_prompt // GRID_W
    lat0 = n_prompt // t_lat
    hd = HEADS_C * DH_C

    def off(b, r):
        return (_window_start(r, n_rows) - r + (WIN_R - 1), 0, 0, 0)

    return pl.pallas_call(
        functools.partial(_nbr_kernel, n_rows=n_rows),
        out_shape=jax.ShapeDtypeStruct((n_lat_b * t_lat, hd), F32),
        grid=(n_lat_b, n_rows),
        in_specs=[pl.BlockSpec((GRID_W, hd), lambda b, r: (row0 + b * n_rows + r, 0)),
                  pl.BlockSpec((t_lat, hd), lambda b, r: (lat0 + b, 0)),
                  pl.BlockSpec((t_lat, hd), lambda b, r: (lat0 + b, 0)),
                  pl.BlockSpec((past, hd), lambda b, r: (b, 0)),
                  pl.BlockSpec((past, hd), lambda b, r: (b, 0)),
                  pl.BlockSpec((1, HEADS_C, GRID_W, wr * GRID_W), off)],
        out_specs=pl.BlockSpec((GRID_W, hd), lambda b, r: (b * n_rows + r, 0)),
        compiler_params=_cparams(2),
        name="nbr_attention",
    )(q, k, v, k_ctx, v_ctx, bias)


def _chunk_masks(d):
    r = lax.broadcasted_iota(jnp.int32, (CHUNK, CHUNK), 0)
    c = lax.broadcasted_iota(jnp.int32, (CHUNK, CHUNK), 1)
    diff = r - c if d == 0 else c - r
    incl = diff >= 0
    strict = diff > 0
    incl_t = diff <= 0
    return incl, strict, incl_t


def _gate_kernel(raw_ref, p0_ref, p1_ref, o_ref, *, kind):
    raw = raw_ref[...]
    lane = lax.broadcasted_iota(jnp.int32, raw.shape, 1)
    first = (lane % 16) < 8
    if kind == "gdn":
        g = -jnp.exp(p0_ref[...]) * jax.nn.softplus(raw + p1_ref[...])
        o_ref[...] = jnp.where(first, g, _sigmoid(raw))
    else:
        x = raw + p0_ref[...]
        o_ref[...] = jnp.where(first, x, -jax.nn.softplus(-x))


def _gates(kind, raw, p0, p1, tm=512):
    m, n = raw.shape
    return pl.pallas_call(
        functools.partial(_gate_kernel, kind=kind),
        out_shape=jax.ShapeDtypeStruct((m, n), F32),
        grid=(m // tm,),
        in_specs=[pl.BlockSpec((tm, n), lambda i: (i, 0)),
                  pl.BlockSpec((1, n), lambda i: (0, 0)),
                  pl.BlockSpec((1, n), lambda i: (0, 0))],
        out_specs=pl.BlockSpec((tm, n), lambda i: (i, 0)),
        compiler_params=_cparams(1),
        name="gates_" + kind,
    )(raw, p0, p1)


def _gate_layouts(g):
    m = g.shape[0]
    col = g.reshape(m // CHUNK, CHUNK, 2, 16).transpose(2, 0, 1, 3)
    return col, col.transpose(0, 1, 3, 2)


def _prep_b_kernel(p_ref, prev_ref, next_ref, w_ref, q_ref, k_ref, v_ref, *, tm, tiles_per_seq):
    i = pl.program_id(0)
    pos = i % tiles_per_seq
    rows = lax.broadcasted_iota(jnp.int32, (tm, D), 0)
    outs = (q_ref, k_ref, v_ref)
    for seg in range(3):
        sl = slice(seg * D, (seg + 1) * D)
        x = p_ref[:, sl]
        prev_row = jnp.where(pos == 0, 0.0, prev_ref[7:8, sl])
        next_row = jnp.where(pos == tiles_per_seq - 1, 0.0, next_ref[0:1, sl])
        x_prev = jnp.where(rows == 0, prev_row, pltpu.roll(x, 1, axis=0))
        x_next = jnp.where(rows == tm - 1, next_row, pltpu.roll(x, tm - 1, axis=0))
        y = x_prev * w_ref[0:1, sl] + x * w_ref[1:2, sl] + x_next * w_ref[2:3, sl]
        y = y * _sigmoid(y)
        if seg < 2:
            for h in range(HEADS_B):
                hs = slice(h * DK_B, (h + 1) * DK_B)
                yh = y[:, hs]
                outs[seg][:, hs] = yh * lax.rsqrt(jnp.sum(yh * yh, axis=-1, keepdims=True) + 1e-6)
        else:
            v_ref[...] = y


def _prep_b(p, w_conv, row0, n_rows, t_seq, tm=256):
    total8 = p.shape[0] // 8
    r0 = row0 // tm
    tiles_per_seq = t_seq // tm
    out = jax.ShapeDtypeStruct((n_rows, D), F32)
    ospec = pl.BlockSpec((tm, D), lambda i: (i, 0))
    return pl.pallas_call(
        functools.partial(_prep_b_kernel, tm=tm, tiles_per_seq=tiles_per_seq),
        out_shape=(out, out, out),
        grid=(n_rows // tm,),
        in_specs=[pl.BlockSpec((tm, p.shape[1]), lambda i: (r0 + i, 0)),
                  pl.BlockSpec((8, p.shape[1]), lambda i: (jnp.maximum((r0 + i) * (tm // 8) - 1, 0), 0)),
                  pl.BlockSpec((8, p.shape[1]), lambda i: (jnp.minimum((r0 + i + 1) * (tm // 8), total8 - 1), 0)),
                  pl.BlockSpec((CONV_W, 3 * D), lambda i: (0, 0))],
        out_specs=(ospec, ospec, ospec),
        compiler_params=_cparams(1),
        name="prep_b",
    )(p, p, p, w_conv)


def _unit_tri_inverses(mats):
    r = lax.broadcasted_iota(jnp.int32, mats[0].shape, 0)
    c = lax.broadcasted_iota(jnp.int32, mats[0].shape, 1)
    eye = jnp.where(r == c, 1.0, 0.0)
    ps = [-a for a in mats]
    ts = [eye + p for p in ps]
    for _ in range(int(np.log2(CHUNK)) - 1):
        ps = [_dot_hi(p, p) for p in ps]
        ts = [t + _dot_hi(t, p) for t, p in zip(ts, ps)]
    return ts


def _gdn_kernel(*refs, n_chunks, zero_init):
    ins, rest = refs[:10], refs[10:]
    if not zero_init:
        s0_ref, rest = rest[0], rest[1:]
    o_refs, (sfin_ref, s_ref) = rest[:2], rest[2:]
    c = pl.program_id(1)

    @pl.when(c == 0)
    def _():
        if zero_init:
            s_ref[...] = jnp.zeros(s_ref.shape, F32)
        else:
            s_ref[...] = s0_ref[0]

    scale = DK_B ** -0.5
    qh, kh, vh, gc_c, b_c, gl, states, decay, strict = [], [], [], [], [], [], [], [], []
    for d in range(2):
        q_ref, k_ref, v_ref, gc_ref, gr_ref = ins[5 * d:5 * d + 5]
        incl, strict_d, incl_t = _chunk_masks(d)
        gcol = gc_ref[0, 0]
        grow = gr_ref[0, 0]
        gcs = _mask_dot(jnp.where(incl, 1.0, 0.0).astype(BF16), gcol)
        grs = _dot_mask(grow, jnp.where(incl_t, 1.0, 0.0).astype(BF16))
        last = CHUNK - 1 if d == 0 else 0
        q_all, k_all, v_all = q_ref[...], k_ref[...], v_ref[...]
        for h in range(HEADS_B):
            hs = slice(h * DK_B, (h + 1) * DK_B)
            qh.append(q_all[:, hs] * scale)
            kh.append(k_all[:, hs])
            vh.append(v_all[:, hs])
            gc_c.append(gcs[:, h:h + 1])
            b_c.append(gcol[:, HEADS_B + h:HEADS_B + h + 1])
            gl.append(gcs[last:last + 1, h:h + 1])
            states.append(s_ref[d, h])
            decay.append(jnp.exp(jnp.where(incl, gcs[:, h:h + 1] - grs[h:h + 1, :], NEG_INF)))
            strict.append(strict_d)
    chains = range(2 * HEADS_B)
    egc = [jnp.exp(g) for g in gc_c]
    kb = [kh[i] * b_c[i] for i in chains]
    a_mat = [jnp.where(strict[i], _dot_nt(kb[i], kh[i]) * decay[i], 0.0) for i in chains]
    attn = [_dot_nt(qh[i], kh[i]) * decay[i] for i in chains]
    o_state = [_dot(qh[i] * egc[i], states[i]) for i in chains]
    t_inv = _unit_tri_inverses(a_mat)
    u = [_dot_hi(t_inv[i], vh[i] * b_c[i]) for i in chains]
    w = [_dot_hi(t_inv[i], kb[i] * egc[i]) for i in chains]
    v_new = [u[i] - _dot(w[i], states[i]) for i in chains]
    outs = [o_state[i] + _dot(attn[i], v_new[i]) for i in chains]
    for d in range(2):
        o_refs[d][...] = jnp.concatenate(outs[d * HEADS_B:(d + 1) * HEADS_B], axis=1)
    for i in chains:
        s_ref[i // HEADS_B, i % HEADS_B] = (states[i] * jnp.exp(gl[i])
                                            + _dot_tn(kh[i] * jnp.exp(gl[i] - gc_c[i]), v_new[i]))

    @pl.when(c == n_chunks - 1)
    def _():
        sfin_ref[0] = s_ref[...]


def _chunk_index(d, c, n_chunks):
    return c + d * (n_chunks - 1 - 2 * c)


def _gdn(q, k, v, gcol, grow, s0, n_seq, t_seq, chunk0):
    n_chunks = t_seq // CHUNK
    in_specs, args, out_specs = [], [], []
    for d in range(2):
        def rows(b, c, d=d):
            return (b * n_chunks + _chunk_index(d, c, n_chunks), 0)

        def gidx(b, c, d=d):
            return (d, chunk0 + b * n_chunks + _chunk_index(d, c, n_chunks), 0, 0)

        blk = pl.BlockSpec((CHUNK, D), rows)
        in_specs += [blk, blk, blk, pl.BlockSpec((1, 1, CHUNK, 16), gidx), pl.BlockSpec((1, 1, 16, CHUNK), gidx)]
        args += [q, k, v, gcol, grow]
        out_specs.append(blk)
    state_spec = pl.BlockSpec((1, 2, HEADS_B, DK_B, DK_B), lambda b, c: (b, 0, 0, 0, 0))
    if s0 is not None:
        in_specs.append(state_spec)
        args.append(s0)
    m = n_seq * t_seq
    return pl.pallas_call(
        functools.partial(_gdn_kernel, n_chunks=n_chunks, zero_init=s0 is None),
        out_shape=(jax.ShapeDtypeStruct((m, D), F32), jax.ShapeDtypeStruct((m, D), F32),
                   jax.ShapeDtypeStruct((n_seq, 2, HEADS_B, DK_B, DK_B), F32)),
        grid=(n_seq, n_chunks),
        in_specs=in_specs,
        out_specs=tuple(out_specs) + (state_spec,),
        scratch_shapes=[pltpu.VMEM((2, HEADS_B, DK_B, DK_B), F32)],
        compiler_params=_cparams(2),
        name="gdn",
    )(*args)


def _mlstm_kernel(*refs, n_chunks, zero_init):
    ins, rest = refs[:8], refs[8:]
    if not zero_init:
        (c0_ref, n0_ref, m0_ref), rest = rest[:3], rest[3:]
    o_refs, (cfin_ref, nfin_ref, mfin_ref, c_ref, n_ref, m_ref) = rest[:2], rest[2:]
    c = pl.program_id(1)

    @pl.when(c == 0)
    def _():
        if zero_init:
            c_ref[...] = jnp.zeros(c_ref.shape, F32)
            n_ref[...] = jnp.zeros(n_ref.shape, F32)
            m_ref[...] = jnp.zeros(m_ref.shape, F32)
        else:
            c_ref[...] = c0_ref[0]
            n_ref[...] = n0_ref[0]
            m_ref[...] = m0_ref[0]

    scale = DK_D ** -0.5
    qh, kh, vh, b_c, ig_c, m_old, n_old, c_old, d_intra, b_last = [], [], [], [], [], [], [], [], [], []
    for d in range(2):
        qk_ref, v_ref, gc_ref, gr_ref = ins[4 * d:4 * d + 4]
        incl, _, incl_t = _chunk_masks(d)
        gcol = gc_ref[0, 0]
        grow = gr_ref[0, 0]
        bcs = _mask_dot(jnp.where(incl, 1.0, 0.0).astype(BF16), gcol)
        brs = _dot_mask(grow, jnp.where(incl_t, 1.0, 0.0).astype(BF16))
        last = CHUNK - 1 if d == 0 else 0
        qk_all, v_all = qk_ref[...], v_ref[...]
        m_all, n_all = m_ref[d], n_ref[d]
        for h in range(HEADS_D):
            qh.append(qk_all[:, h * DK_D:(h + 1) * DK_D] * scale)
            kh.append(qk_all[:, HEADS_D * DK_D + h * DK_D: HEADS_D * DK_D + (h + 1) * DK_D])
            vh.append(v_all[:, h * DV_D:(h + 1) * DV_D])
            b_c.append(bcs[:, HEADS_D + h:HEADS_D + h + 1])
            ig_c.append(gcol[:, h:h + 1])
            m_old.append(m_all[h:h + 1, 0:1])
            n_old.append(n_all[h:h + 1, :])
            c_old.append(c_ref[d, h])
            d_intra.append(jnp.where(incl, bcs[:, HEADS_D + h:HEADS_D + h + 1]
                                     - brs[HEADS_D + h:HEADS_D + h + 1, :] + grow[h:h + 1, :], NEG_INF))
            b_last.append(bcs[last:last + 1, HEADS_D + h:HEADS_D + h + 1])
    heads = range(2 * HEADS_D)
    a_inter = [b_c[h] + m_old[h] for h in heads]
    m_t = [jnp.maximum(a_inter[h], jnp.max(d_intra[h], axis=-1, keepdims=True)) for h in heads]
    w_inter = [jnp.exp(a_inter[h] - m_t[h]) for h in heads]
    s = [_dot_nt(qh[h], kh[h]) * jnp.exp(d_intra[h] - m_t[h]) for h in heads]
    q_state = [_dot(qh[h], c_old[h]) for h in heads]
    d_state = [b_last[h] - b_c[h] + ig_c[h] for h in heads]
    m_new = [jnp.maximum(b_last[h] + m_old[h], jnp.max(d_state[h], axis=0, keepdims=True)) for h in heads]
    w_old = [jnp.exp(b_last[h] + m_old[h] - m_new[h]) for h in heads]
    kw = [kh[h] * jnp.exp(d_state[h] - m_new[h]) for h in heads]
    kv = [_dot_tn(kw[h], vh[h]) for h in heads]
    outs = []
    for h in heads:
        num = w_inter[h] * q_state[h] + _dot(s[h], vh[h])
        qn = jnp.sum(qh[h] * n_old[h], axis=-1, keepdims=True)
        den = w_inter[h] * qn + jnp.sum(s[h], axis=-1, keepdims=True)
        outs.append(num / jnp.maximum(jnp.abs(den), jnp.exp(-m_t[h])))
    for h in heads:
        c_ref[h // HEADS_D, h % HEADS_D] = w_old[h] * c_old[h] + kv[h]
    for d in range(2):
        hd = range(d * HEADS_D, (d + 1) * HEADS_D)
        o_refs[d][...] = jnp.concatenate([outs[h] for h in hd], axis=1)
        n_ref[d] = jnp.concatenate([w_old[h] * n_old[h] + jnp.sum(kw[h], axis=0, keepdims=True) for h in hd], axis=0)
        m_ref[d] = jnp.concatenate([jnp.broadcast_to(m_new[h], (1, m_ref.shape[2])) for h in hd], axis=0)

    @pl.when(c == n_chunks - 1)
    def _():
        cfin_ref[0] = c_ref[...]
        nfin_ref[0] = n_ref[...]
        mfin_ref[0] = m_ref[...]


def _mlstm(p, gcol, grow, states, n_seq, t_seq, row0):
    n_chunks = t_seq // CHUNK
    chunk0 = row0 // CHUNK

    in_specs, args, out_specs = [], [], []
    for d in range(2):
        def rows(b, c, d=d):
            return chunk0 + b * n_chunks + _chunk_index(d, c, n_chunks)

        def gidx(b, c, d=d, rows=rows):
            return (d, rows(b, c), 0, 0)

        in_specs += [pl.BlockSpec((CHUNK, D), lambda b, c, rows=rows: (rows(b, c), 0)),
                     pl.BlockSpec((CHUNK, D), lambda b, c, rows=rows: (rows(b, c), 1)),
                     pl.BlockSpec((1, 1, CHUNK, 16), gidx),
                     pl.BlockSpec((1, 1, 16, CHUNK), gidx)]
        args += [p, p, gcol, grow]
        out_specs.append(pl.BlockSpec((CHUNK, D), lambda b, c, d=d: (b * n_chunks + _chunk_index(d, c, n_chunks), 0)))
    c_spec = pl.BlockSpec((1, 2, HEADS_D, DK_D, DV_D), lambda b, c: (b, 0, 0, 0, 0))
    n_spec = pl.BlockSpec((1, 2, HEADS_D, DK_D), lambda b, c: (b, 0, 0, 0))
    m_spec = pl.BlockSpec((1, 2, HEADS_D, 128), lambda b, c: (b, 0, 0, 0))
    if states is not None:
        in_specs += [c_spec, n_spec, m_spec]
        args += list(states)
    m = n_seq * t_seq
    return pl.pallas_call(
        functools.partial(_mlstm_kernel, n_chunks=n_chunks, zero_init=states is None),
        out_shape=(jax.ShapeDtypeStruct((m, D), F32), jax.ShapeDtypeStruct((m, D), F32),
                   jax.ShapeDtypeStruct((n_seq, 2, HEADS_D, DK_D, DV_D), F32),
                   jax.ShapeDtypeStruct((n_seq, 2, HEADS_D, DK_D), F32),
                   jax.ShapeDtypeStruct((n_seq, 2, HEADS_D, 128), F32)),
        grid=(n_seq, n_chunks),
        in_specs=in_specs,
        out_specs=tuple(out_specs) + (c_spec, n_spec, m_spec),
        scratch_shapes=[pltpu.VMEM((2, HEADS_D, DK_D, DV_D), F32),
                        pltpu.VMEM((2, HEADS_D, DK_D), F32),
                        pltpu.VMEM((2, HEADS_D, 128), F32)],
        compiler_params=_cparams(2),
        name="mlstm",
    )(*args)


def _router_kernel(x_ref, sc_ref, sh_ref, wt_ref, b_ref, xs_ref, pos_ref, gate_ref, cnt_ref):
    h = (x_ref[...] * (1.0 + sc_ref[0]) + sh_ref[0]).astype(BF16)
    logits = _dot_nt(wt_ref[...], h) + b_ref[...]
    sub = lax.broadcasted_iota(jnp.int32, logits.shape, 0)
    onehots, vals = [], []
    for k in range(TOP_K):
        m = jnp.max(logits, axis=0, keepdims=True)
        idx = jnp.min(jnp.where(logits == m, sub, N_EXPERTS), axis=0, keepdims=True)
        hit = sub == idx
        logits = jnp.where(hit, -jnp.inf, logits)
        onehots.append(jnp.where(hit, 1.0, 0.0))
        vals.append(m)
    exps = [jnp.exp(v - vals[0]) for v in vals]
    den = exps[0] + exps[1] + exps[2] + exps[3]
    counts = [jnp.sum(o, axis=1, keepdims=True) for o in onehots]
    n = counts[0] + counts[1] + counts[2] + counts[3]
    n_al = jnp.floor((n + (SEG_ALIGN - 1)) * (1.0 / SEG_ALIGN)) * SEG_ALIGN
    er = lax.broadcasted_iota(jnp.int32, (N_EXPERTS, N_EXPERTS), 0)
    ec = lax.broadcasted_iota(jnp.int32, (N_EXPERTS, N_EXPERTS), 1)
    before_e = jnp.where(ec < er, 1.0, 0.0).astype(BF16)
    start = jnp.dot(before_e, jnp.broadcast_to(n_al, logits.shape).astype(BF16), preferred_element_type=F32)
    tr = lax.broadcasted_iota(jnp.int32, (TILE_TOK, TILE_TOK), 0)
    tc = lax.broadcasted_iota(jnp.int32, (TILE_TOK, TILE_TOK), 1)
    before_t = jnp.where(tr < tc, 1.0, 0.0).astype(BF16)
    rows = lax.broadcasted_iota(jnp.int32, (TILE_ROWS, TILE_TOK), 0)
    perm = jnp.zeros((TILE_ROWS, TILE_TOK), F32)
    prev = jnp.zeros_like(n)
    for k in range(TOP_K):
        cum = jnp.dot(onehots[k].astype(BF16), before_t, preferred_element_type=F32)
        pos = jnp.sum(onehots[k] * (start + prev + cum), axis=0, keepdims=True).astype(jnp.int32)
        prev = prev + counts[k]
        pos_ref[0, k:k + 1, :] = pos
        gate_ref[0, k:k + 1, :] = exps[k] / den
        perm = perm + jnp.where(rows == pos, 1.0, 0.0)
    xs_ref[...] = jnp.dot(perm.astype(BF16), h, preferred_element_type=F32)
    cnt_ref[0] = jnp.broadcast_to(n_al, cnt_ref.shape[1:]).astype(jnp.int32)


def _router(x, mod, w, b, n_prompt, t_lat):
    m = x.shape[0]
    nt = m // TILE_TOK
    cidx = functools.partial(_cond_index, tm=TILE_TOK, n_prompt=n_prompt, t_lat=t_lat)
    sel = pl.BlockSpec((1, TOP_K, TILE_TOK), lambda i: (i, 0, 0))
    return pl.pallas_call(
        _router_kernel,
        out_shape=(jax.ShapeDtypeStruct((nt * TILE_ROWS, D), F32),
                   jax.ShapeDtypeStruct((nt, TOP_K, TILE_TOK), jnp.int32),
                   jax.ShapeDtypeStruct((nt, TOP_K, TILE_TOK), F32),
                   jax.ShapeDtypeStruct((nt, N_EXPERTS, 128), jnp.int32)),
        grid=(nt,),
        in_specs=[pl.BlockSpec((TILE_TOK, D), lambda i: (i, 0)),
                  pl.BlockSpec((1, 1, D), lambda i: (cidx(i), 0, 4)),
                  pl.BlockSpec((1, 1, D), lambda i: (cidx(i), 0, 3)),
                  pl.BlockSpec((N_EXPERTS, D), lambda i: (0, 0)),
                  pl.BlockSpec((N_EXPERTS, 1), lambda i: (0, 0))],
        out_specs=(pl.BlockSpec((TILE_ROWS, D), lambda i: (i, 0)), sel, sel,
                   pl.BlockSpec((1, N_EXPERTS, 128), lambda i: (i, 0, 0))),
        compiler_params=_cparams(1),
        name="router",
    )(x, mod, mod, w.T, b.reshape(N_EXPERTS, 1))


def _expert_kernel(be_ref, boff_ref, brows_ref, tlo_ref, thi_ref, nb_ref, scum_ref, sn_ref, ssrc_ref, tused_ref,
                   xs_hbm, win_ref, bin_ref, wout_ref, bout_ref, ys_hbm,
                   xbuf, ybuf, sem_in, sem_out, winbf, woutbf, *, n_tiles):
    i = pl.program_id(0)
    nb = nb_ref[0]
    slot = i % 2

    def for_pieces(b, fn):
        e = be_ref[b]
        off = boff_ref[b]

        def body(t, carry):
            j = e * n_tiles + t
            s0 = scum_ref[j]
            lo = jnp.maximum(s0, off)
            hi = jnp.minimum(s0 + sn_ref[j], off + EXPERT_ROWS)

            @pl.when(hi > lo)
            def _():
                fn(pl.multiple_of(ssrc_ref[j] + (lo - s0), SEG_ALIGN), pl.multiple_of(lo - off, SEG_ALIGN),
                   pl.multiple_of(hi - lo, SEG_ALIGN))
            return carry
        lax.fori_loop(tlo_ref[b], thi_ref[b], body, 0)

    def in_copy(s, src, dst, n):
        return pltpu.make_async_copy(xs_hbm.at[pl.ds(src, n), :], xbuf.at[s, pl.ds(dst, n), :], sem_in.at[s])

    def out_copy(s, src, dst, n):
        return pltpu.make_async_copy(ybuf.at[s, pl.ds(dst, n), :], ys_hbm.at[pl.ds(src, n), :], sem_out.at[s])

    def wait_block(copy, b, s):
        n = pl.multiple_of(brows_ref[b], SEG_ALIGN)

        @pl.when(n > 0)
        def _():
            copy(s, 0, 0, n).wait()

    def tail_copy(t):
        used = pl.multiple_of(tused_ref[t], SEG_ALIGN)
        n = pl.multiple_of(TILE_ROWS - used, SEG_ALIGN)
        return n, pltpu.make_async_copy(ybuf.at[1, pl.ds(0, n), :],
                                        ys_hbm.at[pl.ds(pl.multiple_of(t * TILE_ROWS + used, SEG_ALIGN), n), :],
                                        sem_out.at[1])

    @pl.when(i == 0)
    def _():
        xbuf[...] = jnp.zeros(xbuf.shape, F32)
        ybuf[1] = jnp.zeros(ybuf.shape[1:], F32)
        for_pieces(0, lambda src, dst, n: in_copy(0, src, dst, n).start())

        def start_tail(t, carry):
            n, copy = tail_copy(t)

            @pl.when(n > 0)
            def _():
                copy.start()
            return carry

        def wait_tail(t, carry):
            n, copy = tail_copy(t)

            @pl.when(n > 0)
            def _():
                copy.wait()
            return carry

        lax.fori_loop(0, n_tiles, start_tail, 0)
        lax.fori_loop(0, n_tiles, wait_tail, 0)

    @pl.when(i + 1 < nb)
    def _():
        for_pieces(i + 1, lambda src, dst, n: in_copy(1 - slot, src, dst, n).start())

    new_expert = jnp.logical_or(i == 0, be_ref[i] != be_ref[jnp.maximum(i - 1, 0)])

    @pl.when(jnp.logical_and(new_expert, i < nb))
    def _():
        winbf[...] = win_ref[0].astype(BF16)
        woutbf[...] = wout_ref[0].astype(BF16)

    @pl.when(i < nb)
    def _():
        wait_block(in_copy, i, slot)
        x = xbuf[slot].astype(BF16)
        hmid = jnp.dot(x, winbf[...], preferred_element_type=F32) + bin_ref[0]
        glu = jnp.minimum(hmid[:, :D_FF], SWIGLU_LIMIT)
        lin = jnp.clip(hmid[:, D_FF:], -SWIGLU_LIMIT, SWIGLU_LIMIT)
        act = glu * _sigmoid(SWIGLU_ALPHA * glu) * (lin + 1.0)
        y = jnp.dot(act.astype(BF16), woutbf[...], preferred_element_type=F32) + bout_ref[0]

        @pl.when(i >= 1)
        def _():
            wait_block(out_copy, i - 1, 1 - slot)

        ybuf[slot] = y
        for_pieces(i, lambda src, dst, n: out_copy(slot, src, dst, n).start())

        @pl.when(i == nb - 1)
        def _():
            wait_block(out_copy, i, slot)


def _experts(xs, tables, layer, w_in, b_in, w_out, b_out, n_tiles):
    n_blocks = tables[0].shape[0]
    depth = w_in.shape[0]

    def wmap(i, be, *_):
        return (layer, be[i], 0, 0)

    grid_spec = pltpu.PrefetchScalarGridSpec(
        num_scalar_prefetch=len(tables),
        grid=(n_blocks,),
        in_specs=[pl.BlockSpec(memory_space=pl.ANY),
                  pl.BlockSpec((None, 1, D, 2 * D_FF), wmap),
                  pl.BlockSpec((None, 1, 1, 2 * D_FF), wmap),
                  pl.BlockSpec((None, 1, D_FF, D), wmap),
                  pl.BlockSpec((None, 1, 1, D), wmap)],
        out_specs=pl.BlockSpec(memory_space=pl.ANY),
        scratch_shapes=[pltpu.VMEM((2, EXPERT_ROWS, D), F32),
                        pltpu.VMEM((2, EXPERT_ROWS, D), F32),
                        pltpu.SemaphoreType.DMA((2,)),
                        pltpu.SemaphoreType.DMA((2,)),
                        pltpu.VMEM((D, 2 * D_FF), BF16),
                        pltpu.VMEM((D_FF, D), BF16)])
    return pl.pallas_call(
        functools.partial(_expert_kernel, n_tiles=n_tiles),
        out_shape=jax.ShapeDtypeStruct(xs.shape, F32),
        grid_spec=grid_spec,
        compiler_params=_cparams(1),
        name="experts",
    )(*tables, xs, w_in, b_in.reshape(depth, N_EXPERTS, 1, 2 * D_FF), w_out, b_out.reshape(depth, N_EXPERTS, 1, D))


def _combine_kernel(ys_ref, pos_ref, gate_ref, x_ref, g_ref, lg_ref, lb_ref, o_ref):
    rows = lax.broadcasted_iota(jnp.int32, (TILE_ROWS, TILE_TOK), 0)
    perm = jnp.zeros((TILE_ROWS, TILE_TOK), F32)
    gmat = jnp.zeros((TILE_ROWS, TILE_TOK), F32)
    for k in range(TOP_K):
        hit = rows == pos_ref[0, k:k + 1, :]
        perm = perm + jnp.where(hit, 1.0, 0.0)
        gmat = gmat + jnp.where(hit, gate_ref[0, k:k + 1, :], 0.0)
    row_gate = jnp.sum(gmat, axis=1, keepdims=True)
    z = ys_ref[...] * row_gate
    hi, lo = _split2(z)
    p = perm.astype(BF16)
    dims = (((0,), (0,)), ((), ()))
    y = (lax.dot_general(p, hi, dims, preferred_element_type=F32)
         + lax.dot_general(p, lo, dims, preferred_element_type=F32))
    o_ref[...] = _layer_norm(_ALPHA * x_ref[...] + g_ref[0] * y, lg_ref[...], lb_ref[...])


def _combine(ys, pos, gates, x, mod, ln_g, ln_b, n_prompt, t_lat):
    m = x.shape[0]
    cidx = functools.partial(_cond_index, tm=TILE_TOK, n_prompt=n_prompt, t_lat=t_lat)
    row = pl.BlockSpec((TILE_TOK, D), lambda i: (i, 0))
    vec = pl.BlockSpec((1, D), lambda i: (0, 0))
    sel = pl.BlockSpec((1, TOP_K, TILE_TOK), lambda i: (i, 0, 0))
    return pl.pallas_call(
        _combine_kernel,
        out_shape=jax.ShapeDtypeStruct((m, D), F32),
        grid=(m // TILE_TOK,),
        in_specs=[pl.BlockSpec((TILE_ROWS, D), lambda i: (i, 0)), sel, sel, row,
                  pl.BlockSpec((1, 1, D), lambda i: (cidx(i), 0, 5)), vec, vec],
        out_specs=row,
        compiler_params=_cparams(1),
        name="combine",
    )(ys, pos, gates, x, mod, ln_g.reshape(1, D), ln_b.reshape(1, D))


def _routing_tables(seg_n):
    n_tiles = seg_n.shape[0]
    seg_src = (jnp.arange(n_tiles, dtype=jnp.int32)[:, None] * TILE_ROWS
               + jnp.cumsum(seg_n, axis=1) - seg_n)
    seg_cum = jnp.cumsum(seg_n, axis=0) - seg_n
    count = jnp.sum(seg_n, axis=0)
    padded = (count + EXPERT_ROWS - 1) // EXPERT_ROWS * EXPERT_ROWS
    pad_end = jnp.cumsum(padded)
    pad_start = pad_end - padded
    n_blocks = n_tiles * TILE_ROWS // EXPERT_ROWS + N_EXPERTS
    n_used = pad_end[-1] // EXPERT_ROWS
    blocks = jnp.arange(n_blocks, dtype=jnp.int32)
    owner = jnp.minimum(jnp.sum(blocks[:, None] * EXPERT_ROWS >= pad_end[None, :], axis=1), N_EXPERTS - 1)
    last_owner = jnp.minimum(jnp.sum((n_used - 1) * EXPERT_ROWS >= pad_end), N_EXPERTS - 1)
    block_expert = jnp.where(blocks < n_used, owner, last_owner).astype(jnp.int32)
    own = block_expert[:, None] == jnp.arange(N_EXPERTS, dtype=jnp.int32)[None, :]
    block_off = blocks * EXPERT_ROWS - jnp.sum(jnp.where(own, pad_start[None, :], 0), axis=1)
    block_rows = jnp.clip(jnp.sum(jnp.where(own, count[None, :], 0), axis=1) - block_off, 0, EXPERT_ROWS)
    cum_b = jnp.sum(jnp.where(own[:, None, :], seg_cum[None], 0), axis=2)
    n_b = jnp.sum(jnp.where(own[:, None, :], seg_n[None], 0), axis=2)
    t_lo = jnp.sum(cum_b + n_b <= block_off[:, None], axis=1)
    t_hi = jnp.sum(cum_b < block_off[:, None] + EXPERT_ROWS, axis=1)
    i32 = lambda a: a.astype(jnp.int32)
    return (block_expert, i32(block_off), i32(block_rows), i32(t_lo), i32(t_hi), i32(n_used).reshape(1),
            i32(seg_cum.T.reshape(-1)), i32(seg_n.T.reshape(-1)), i32(seg_src.T.reshape(-1)),
            i32(jnp.sum(seg_n, axis=1)))


def _moe(x, mod, layer, w_router, b_router, w_in, b_in, w_out, b_out, ln_g, ln_b, n_prompt, t_lat):
    xs, pos, gates, cnt = _router(x, mod, w_router, b_router, n_prompt, t_lat)
    tables = _routing_tables(cnt[:, :, 0])
    ys = _experts(xs, tables, layer, w_in, b_in, w_out, b_out, cnt.shape[0])
    return _combine(ys, pos, gates, x, mod, ln_g, ln_b, n_prompt, t_lat)


def kernel(x_prompt, x_sample, cache_k_a, cache_v_a, state_b, cache_k_c, cache_v_c, state_d_c, state_d_n, state_d_m, c, c_ctx, w_mod, b_mod, ln_g, ln_b, w_in_a, q_norm_a, k_norm_a, w_out_a, w_in_b, w_conv_b, w_gate_b, a_log_b, dt_bias_b, o_norm_b, w_out_b, w_in_c, rpb_c, w_out_c, w_in_d, w_gate_d, b_gate_d, o_norm_d, w_out_d, w_router, b_router, w_exp_in, b_exp_in, w_exp_out, b_exp_out):
    bp, t_p, _ = x_prompt.shape
    bs, t_s, _ = x_sample.shape
    past = cache_k_a.shape[2]
    depth = w_mod.shape[0]
    assert depth == 4 and bs + 1 <= N_COND
    n_p, n_s = bp * t_p, bs * t_s
    x = jnp.concatenate([x_prompt.reshape(n_p, D), x_sample.reshape(n_s, D)], axis=0)
    cond = jnp.concatenate([c_ctx.reshape(1, D), c, jnp.zeros((N_COND - 1 - bs, D), F32)], axis=0)
    res = functools.partial(_proj_res_ln, n_prompt=n_p, t_lat=t_s)
    outs = {}
    for i in range(depth):
        kind, j = i % 4, i // 4
        mod = _modulation_table(cond, w_mod, b_mod[i], i).reshape(N_COND, 1, 6 * D)
        proj = functools.partial(_modmm, x, mod, 1, 0, n_prompt=n_p, t_lat=t_s)
        if kind == 0:
            p = proj(w_in_a[j])
            q, k = _prep_a(p, q_norm_a[j], k_norm_a[j], n_p, t_s)
            hq, hk = HEADS_A * DH_A, KV_A * DH_A
            vcol = hq // hk + 1
            o_p = _attention(q, pl.BlockSpec((t_p, hq), lambda b: (b, 0)),
                             [(k, pl.BlockSpec((t_p, hk), lambda b: (b, 0)),
                               p, pl.BlockSpec((t_p, hk), lambda b: (b, vcol)))],
                             n_p, (bp,), HEADS_A, KV_A, DH_A, pl.BlockSpec((t_p, hq), lambda b: (b, 0)))
            tq = 256
            qb0, lat0 = n_p // tq, n_p // t_s
            kc = cache_k_a[:, j].reshape(bs * past, hk)
            vc = cache_v_a[:, j].reshape(bs * past, hk)
            o_s = _attention(q, pl.BlockSpec((tq, hq), lambda b, t: (qb0 + b * (t_s // tq) + t, 0)),
                             [(kc, pl.BlockSpec((past, hk), lambda b, t: (b, 0)),
                               vc, pl.BlockSpec((past, hk), lambda b, t: (b, 0))),
                              (k, pl.BlockSpec((t_s, hk), lambda b, t: (lat0 + b, 0)),
                               p, pl.BlockSpec((t_s, hk), lambda b, t: (lat0 + b, vcol)))],
                             n_s, (bs, t_s // tq), HEADS_A, KV_A, DH_A,
                             pl.BlockSpec((tq, hq), lambda b, t: (b * (t_s // tq) + t, 0)))
            outs["k_a"] = k[:n_p].reshape(bp, 1, t_p, KV_A, DH_A)
            outs["v_a"] = p[:n_p, hq + hk:].reshape(bp, 1, t_p, KV_A, DH_A)
            x = res("plain", (o_p, o_s), w_out_a[j], x, mod, 2, ln_g[i, 0], ln_b[i, 0])
        elif kind == 1:
            p = proj(w_in_b[j])
            raw = proj(w_gate_b[j])
            zeros8 = jnp.zeros((2, HEADS_B), F32)
            a32 = jnp.concatenate([a_log_b[j], zeros8], axis=1).reshape(1, 32)
            dt32 = jnp.concatenate([dt_bias_b[j], zeros8], axis=1).reshape(1, 32)
            gcol, grow = _gate_layouts(_gates("gdn", raw, a32, dt32))
            qp, kp, vp = _prep_b(p, w_conv_b[j], 0, n_p, t_p)
            qs, ks, vs = _prep_b(p, w_conv_b[j], n_p, n_s, t_s)
            o_pf, o_pb, s_p = _gdn(qp, kp, vp, gcol, grow, None, bp, t_p, 0)
            o_sf, o_sb, _ = _gdn(qs, ks, vs, gcol, grow, state_b[:, j], bs, t_s, n_p // CHUNK)
            outs["s_b"] = s_p.reshape(bp, 1, 2, HEADS_B, DK_B, DK_B)
            x = res("silu", (o_pf, o_pb, o_sf, o_sb, p, o_norm_b[j].reshape(1, DK_B)), w_out_b[j], x, mod, 2, ln_g[i, 0], ln_b[i, 0],
                    z_colblock=3)
        elif kind == 2:
            hd = HEADS_C * DH_C
            pq, pk, pv = (proj(w_in_c[j][:, s * hd:(s + 1) * hd]) for s in range(3))
            blk = pl.BlockSpec((t_p, hd), lambda b: (b, 0))
            o_p = _attention(pq, blk, [(pk, blk, pv, blk)], n_p, (bp,), HEADS_C, HEADS_C, DH_C, blk)
            o_s = _nbr_attention(pq, pk, pv, cache_k_c[:, j].reshape(bs * past, hd),
                                 cache_v_c[:, j].reshape(bs * past, hd), rpb_c[j], n_p, bs, t_s, past)
            outs["k_c"] = pk[:n_p].reshape(bp, 1, t_p, HEADS_C, DH_C)
            outs["v_c"] = pv[:n_p].reshape(bp, 1, t_p, HEADS_C, DH_C)
            x = res("plain", (o_p, o_s), w_out_c[j], x, mod, 2, ln_g[i, 0], ln_b[i, 0])
        else:
            p = proj(w_in_d[j])
            raw = proj(w_gate_d[j])
            gcol, grow = _gate_layouts(_gates("mlstm", raw, b_gate_d[j].reshape(1, 32), jnp.zeros((1, 32), F32)))
            h_pf, h_pb, c_p, n_fin, m_fin = _mlstm(p, gcol, grow, None, bp, t_p, 0)
            m0 = jnp.broadcast_to(state_d_m[:, j][..., None], (bs, 2, HEADS_D, 128))
            h_sf, h_sb, _, _, _ = _mlstm(p, gcol, grow, (state_d_c[:, j], state_d_n[:, j], m0), bs, t_s, n_p)
            outs["c_d"] = c_p.reshape(bp, 1, 2, HEADS_D, DK_D, DV_D)
            outs["n_d"] = n_fin.reshape(bp, 1, 2, HEADS_D, DK_D)
            outs["m_d"] = m_fin[..., 0].reshape(bp, 1, 2, HEADS_D)
            x = res("sigmoid", (h_pf, h_pb, h_sf, h_sb, p, o_norm_d[j].reshape(1, DV_D)), w_out_d[j], x, mod, 2, ln_g[i, 0], ln_b[i, 0],
                    z_colblock=2)
        x = _moe(x, mod, i, w_router[i], b_router[i], w_exp_in, b_exp_in, w_exp_out, b_exp_out,
                 ln_g[i, 1], ln_b[i, 1], n_p, t_s)
    return (x[:n_p].reshape(bp, t_p, D), x[n_p:].reshape(bs, t_s, D), outs["k_a"], outs["v_a"], outs["s_b"],
            outs["k_c"], outs["v_c"], outs["c_d"], outs["n_d"], outs["m_d"])
```
